```python
import math
import jax, jax.numpy as jnp
from jax import lax
import numpy as np

D_MODEL = 2048
BATCH = 4
SEQ = 4096
DEPTH = 2

D_MIX = D_MODEL
D_HYENA = D_MIX // 4
D_SCONV = D_MIX // 4
N_HEADS = 16
N_KV_HEADS = 4
GROUP = N_HEADS // N_KV_HEADS
HEAD_DIM = 64
D_ATTN = N_HEADS * HEAD_DIM
D_KV = N_KV_HEADS * HEAD_DIM
SPLIT_SC = 3 * D_HYENA
SPLIT_Q = SPLIT_SC + 3 * D_SCONV
SPLIT_K = SPLIT_Q + D_ATTN
SPLIT_V = SPLIT_K + D_KV
D_IN_PROJ = SPLIT_V + D_KV
D_FF = 5632
WINDOW = 128
BLOCK = 128
CONV_WIDTH = 3
FILTER_BANDS = 16
FILTER_EMB_DIM = 1 + 2 * FILTER_BANDS
FILTER_HIDDEN = 64
HYENA_FAST_DECAY = 0.3
HYENA_SLOW_DECAY = 1.5
HYENA_DECAY_TARGET = 1e-2
HYENA_MIN_DECAY = math.log(HYENA_DECAY_TARGET) / HYENA_SLOW_DECAY
HYENA_MAX_DECAY = math.log(HYENA_DECAY_TARGET) / HYENA_FAST_DECAY
HYENA_SHIFT = 0.05
EPS = 1e-6
NEG_INF = -1e30

kernel_name = "hybrid_hyena_shortconv_swa_macaron"


def rms_norm(x, g):
    xf = x.astype(jnp.float32)
    y = xf * lax.rsqrt(jnp.mean(xf * xf, axis=-1, keepdims=True) + EPS)
    return (y * g.astype(jnp.float32)).astype(x.dtype)


def swiglu(x, w_gate, w_up, w_down):
    return (jax.nn.silu(x @ w_gate) * (x @ w_up)) @ w_down


def centered_conv3(u, w):
    up = jnp.pad(u, ((0, 0), (1, 1), (0, 0)))
    return up[:, :-2] * w[0] + up[:, 1:-1] * w[1] + up[:, 2:] * w[2]


def hyena_filter_kernel(L, w1, b1, w2, b2, w3, b3, freq, w_out):
    f32 = jnp.float32
    n = jnp.arange(L, dtype=f32)
    t = n / float(max(L - 1, 1))
    bands = jnp.linspace(1e-4, FILTER_BANDS - 1, FILTER_BANDS, dtype=f32)
    ang = (2.0 * math.pi / L) * n[:, None] * bands[None, :]
    z = jnp.concatenate([t[:, None], jnp.cos(ang), jnp.sin(ang)], axis=-1)
    fr = freq.astype(f32)
    h = jnp.sin(fr * (z @ w1.astype(f32) + b1.astype(f32)))
    h = jnp.sin(fr * (h @ w2.astype(f32) + b2.astype(f32)))
    h = jnp.sin(fr * (h @ w3.astype(f32) + b3.astype(f32)))
    h = h @ w_out.astype(f32)
    deltas = jnp.abs(jnp.linspace(HYENA_MIN_DECAY, HYENA_MAX_DECAY, D_HYENA, dtype=f32))
    window = jnp.exp(-t[:, None] * deltas[None, :]) + HYENA_SHIFT
    h_fwd = h[:, :D_HYENA] * window
    h_bwd = h[:, D_HYENA:] * window
    return jnp.concatenate([h_fwd, jnp.zeros((1, D_HYENA), f32), h_bwd[:0:-1]], axis=0)


def two_sided_fftconv(u, kernel):
    L = u.shape[1]
    uf = jnp.fft.rfft(u.astype(jnp.float32), n=2 * L, axis=1)
    kf = jnp.fft.rfft(kernel, n=2 * L, axis=0)
    return jnp.fft.irfft(uf * kf[None], n=2 * L, axis=1)[:, :L]


def hyena_mixer(p, short_w, short_b, bias_d, w1, b1, w2, b2, w3, b3, freq, w_out):
    L = p.shape[1]
    u = centered_conv3(p, short_w) + short_b
    x0, x1, v = jnp.split(u, 3, axis=-1)
    zv = v * x1
    kernel = hyena_filter_kernel(L, w1, b1, w2, b2, w3, b3, freq, w_out)
    y = two_sided_fftconv(zv, kernel) + zv.astype(jnp.float32) * bias_d.astype(jnp.float32)
    return x0 * y.astype(x0.dtype)


def short_conv_mixer(p, conv_w):
    gb, gc, hv = jnp.split(p, 3, axis=-1)
    return gb * centered_conv3(gc * hv, conv_w)


def band_blocks(t, nb):
    tp = jnp.pad(t, ((0, 0), (BLOCK, BLOCK), (0, 0), (0, 0)))
    tp = tp.reshape(t.shape[0], nb + 2, BLOCK, t.shape[2], t.shape[3])
    return jnp.concatenate([tp[:, :-2], tp[:, 1:-1], tp[:, 2:]], axis=2)


def windowed_gqa(q, k, v, q_g, k_g, sink):
    f32 = jnp.float32
    Bsz, L, _ = q.shape
    nb = L // BLOCK
    q = rms_norm(q.reshape(Bsz, L, N_HEADS, HEAD_DIM), q_g)
    k = rms_norm(k.reshape(Bsz, L, N_KV_HEADS, HEAD_DIM), k_g)
    v = v.reshape(Bsz, L, N_KV_HEADS, HEAD_DIM)
    qb = q.reshape(Bsz, nb, BLOCK, N_KV_HEADS, GROUP, HEAD_DIM).astype(f32)
    kb = band_blocks(k, nb).astype(f32)
    vb = band_blocks(v, nb).astype(f32)
    scores = jnp.einsum("bnqhgd,bnshd->bnhgqs", qb, kb) * (HEAD_DIM ** -0.5)
    qpos = jnp.arange(nb)[:, None] * BLOCK + jnp.arange(BLOCK)[None, :]
    kpos = (jnp.arange(nb)[:, None] - 1) * BLOCK + jnp.arange(3 * BLOCK)[None, :]
    dist = jnp.abs(qpos[:, :, None] - kpos[:, None, :])
    valid = (dist <= WINDOW) & (kpos[:, None, :] >= 0) & (kpos[:, None, :] < L)
    slopes = jnp.exp2(-8.0 * jnp.arange(1, N_HEADS + 1, dtype=f32) / N_HEADS)
    slopes = slopes.reshape(N_KV_HEADS, GROUP)[None, None, :, :, None, None]
    logits = scores - slopes * dist.astype(f32)[None, :, None, None]
    logits = jnp.where(valid[None, :, None, None], logits, NEG_INF)
    sink_b = sink.astype(f32).reshape(N_KV_HEADS, GROUP)[None, None, :, :, None]
    m = jnp.maximum(jnp.max(logits, axis=-1), sink_b)
    p = jnp.exp(logits - m[..., None])
    denom = jnp.sum(p, axis=-1) + jnp.exp(sink_b - m)
    probs = p / denom[..., None]
    o = jnp.einsum("bnhgqs,bnshd->bnqhgd", probs, vb)
    return o.reshape(Bsz, L, D_ATTN).astype(q.dtype)


def setup_inputs(seed: int = 0) -> dict:
    key = jax.random.key(seed)
    ks = jax.random.split(key, 28)
    f32 = jnp.float32

    def nrm(k, shape, scale):
        return scale * jax.random.normal(k, shape, f32)

    def gain(k, shape):
        return 1.0 + 0.02 * jax.random.normal(k, shape, f32)

    L_ = DEPTH
    return {
        "x": nrm(ks[0], (BATCH, SEQ, D_MODEL), 1.0),
        "norm_ffn1": gain(ks[1], (L_, D_MODEL)),
        "ffn1_w_gate": nrm(ks[2], (L_, D_MODEL, D_FF), D_MODEL ** -0.5),
        "ffn1_w_up": nrm(ks[3], (L_, D_MODEL, D_FF), D_MODEL ** -0.5),
        "ffn1_w_down": nrm(ks[4], (L_, D_FF, D_MODEL), D_FF ** -0.5),
        "norm_mix": gain(ks[5], (L_, D_MODEL)),
        "w_in": nrm(ks[6], (L_, D_MODEL, D_IN_PROJ), D_MODEL ** -0.5),
        "hyena_short_w": nrm(ks[7], (L_, CONV_WIDTH, 3 * D_HYENA), CONV_WIDTH ** -0.5),
        "hyena_short_b": nrm(ks[8], (L_, 3 * D_HYENA), 0.02),
        "filt_w1": nrm(ks[9], (L_, FILTER_EMB_DIM, FILTER_HIDDEN), FILTER_EMB_DIM ** -0.5),
        "filt_b1": nrm(ks[10], (L_, FILTER_HIDDEN), 0.1),
        "filt_w2": nrm(ks[11], (L_, FILTER_HIDDEN, FILTER_HIDDEN), FILTER_HIDDEN ** -0.5),
        "filt_b2": nrm(ks[12], (L_, FILTER_HIDDEN), 0.1),
        "filt_w3": nrm(ks[13], (L_, FILTER_HIDDEN, FILTER_HIDDEN), FILTER_HIDDEN ** -0.5),
        "filt_b3": nrm(ks[14], (L_, FILTER_HIDDEN), 0.1),
        "filt_freq": gain(ks[15], (L_, FILTER_HIDDEN)),
        "filt_w_out": nrm(ks[16], (L_, FILTER_HIDDEN, 2 * D_HYENA), FILTER_HIDDEN ** -0.5),
        "hyena_bias": nrm(ks[17], (L_, D_HYENA), 1.0),
        "sconv_w": nrm(ks[18], (L_, CONV_WIDTH, D_SCONV), CONV_WIDTH ** -0.5),
        "q_norm_g": gain(ks[19], (L_, HEAD_DIM)),
        "k_norm_g": gain(ks[20], (L_, HEAD_DIM)),
        "attn_sink": nrm(ks[21], (L_, N_HEADS), 0.5),
        "mix_out_g": gain(ks[22], (L_, D_MIX)),
        "w_out": nrm(ks[23], (L_, D_MIX, D_MODEL), D_MIX ** -0.5),
        "norm_ffn2": gain(ks[24], (L_, D_MODEL)),
        "ffn2_w_gate": nrm(ks[25], (L_, D_MODEL, D_FF), D_MODEL ** -0.5),
        "ffn2_w_up": nrm(ks[26], (L_, D_MODEL, D_FF), D_MODEL ** -0.5),
        "ffn2_w_down": nrm(ks[27], (L_, D_FF, D_MODEL), D_FF ** -0.5),
    }


def reference(x, norm_ffn1, ffn1_w_gate, ffn1_w_up, ffn1_w_down, norm_mix, w_in,
              hyena_short_w, hyena_short_b, filt_w1, filt_b1, filt_w2, filt_b2,
              filt_w3, filt_b3, filt_freq, filt_w_out, hyena_bias, sconv_w,
              q_norm_g, k_norm_g, attn_sink, mix_out_g, w_out, norm_ffn2,
              ffn2_w_gate, ffn2_w_up, ffn2_w_down):
    for l in range(DEPTH):
        h = rms_norm(x, norm_ffn1[l])
        x = x + 0.5 * swiglu(h, ffn1_w_gate[l], ffn1_w_up[l], ffn1_w_down[l])

        h = rms_norm(x, norm_mix[l])
        proj = h @ w_in[l]
        p_hy = proj[..., :SPLIT_SC]
        p_sc = proj[..., SPLIT_SC:SPLIT_Q]
        q = proj[..., SPLIT_Q:SPLIT_K]
        k = proj[..., SPLIT_K:SPLIT_V]
        v = proj[..., SPLIT_V:]
        y_hy = hyena_mixer(p_hy, hyena_short_w[l], hyena_short_b[l], hyena_bias[l],
                           filt_w1[l], filt_b1[l], filt_w2[l], filt_b2[l],
                           filt_w3[l], filt_b3[l], filt_freq[l], filt_w_out[l])
        y_sc = short_conv_mixer(p_sc, sconv_w[l])
        y_at = windowed_gqa(q, k, v, q_norm_g[l], k_norm_g[l], attn_sink[l])
        g = mix_out_g[l]
        y = jnp.concatenate([
            rms_norm(y_hy, g[:D_HYENA]),
            rms_norm(y_sc, g[D_HYENA:D_HYENA + D_SCONV]),
            rms_norm(y_at, g[D_HYENA + D_SCONV:]),
        ], axis=-1)
        x = x + y @ w_out[l]

        h = rms_norm(x, norm_ffn2[l])
        x = x + 0.5 * swiglu(h, ffn2_w_gate[l], ffn2_w_up[l], ffn2_w_down[l])
    return x
```

```python
import functools
import math

import jax
import jax.numpy as jnp
from jax import lax
from jax.experimental import pallas as pl
from jax.experimental.pallas import tpu as pltpu

F32 = jnp.float32
BF16 = jnp.bfloat16

EPS = 1e-6
NEG_INF = -1e30

D_HYENA = 512
D_SCONV = 512
N_HEADS = 16
N_KV_HEADS = 4
GROUP = N_HEADS // N_KV_HEADS
HEAD_DIM = 64
D_ATTN = N_HEADS * HEAD_DIM
D_KV = N_KV_HEADS * HEAD_DIM
WINDOW = 128
BLOCK = 128
FILTER_BANDS = 16
FILTER_HIDDEN = 64
HYENA_MIN_DECAY = math.log(1e-2) / 1.5
HYENA_MAX_DECAY = math.log(1e-2) / 0.3
HYENA_SHIFT = 0.05

V7X_VMEM_LIMIT_BYTES = 60 * 1024 * 1024
SUBLANES = 8


def _params(*sem):
    return pltpu.CompilerParams(dimension_semantics=sem, vmem_limit_bytes=V7X_VMEM_LIMIT_BYTES)


def _rms(x, g):
    ms = jnp.mean(x * x, axis=-1, keepdims=True)
    return x * lax.rsqrt(ms + EPS) * g


def _ffn_kernel(x_ref, g_ref, wg_ref, wu_ref, wd_ref, o_ref, h_ref):
    @pl.when(pl.program_id(1) == 0)
    def _():
        x = x_ref[...]
        h_ref[...] = _rms(x, g_ref[...]).astype(BF16)
        o_ref[...] = x

    h = h_ref[...]
    gate = jnp.dot(h, wg_ref[...], preferred_element_type=F32)
    up = jnp.dot(h, wu_ref[...], preferred_element_type=F32)
    a = (gate * jax.nn.sigmoid(gate) * up * 0.5).astype(BF16)
    o_ref[...] += jnp.dot(a, wd_ref[...], preferred_element_type=F32)


def _ffn(x, g, wg, wu, wd, *, tm, tf):
    n, d = x.shape
    ff = wg.shape[1]
    return pl.pallas_call(
        _ffn_kernel,
        grid=(n // tm, ff // tf),
        in_specs=[
            pl.BlockSpec((tm, d), lambda i, j: (i, 0)),
            pl.BlockSpec((1, d), lambda i, j: (0, 0)),
            pl.BlockSpec((d, tf), lambda i, j: (0, j)),
            pl.BlockSpec((d, tf), lambda i, j: (0, j)),
            pl.BlockSpec((tf, d), lambda i, j: (j, 0)),
        ],
        out_specs=pl.BlockSpec((tm, d), lambda i, j: (i, 0)),
        out_shape=jax.ShapeDtypeStruct((n, d), F32),
        scratch_shapes=[pltpu.VMEM((tm, d), BF16)],
        compiler_params=_params("parallel", "arbitrary"),
        name="ffn",
    )(x, g, wg, wu, wd)


def _inproj_kernel(x_ref, g_ref, w_ref, o_ref, h_ref):
    @pl.when(pl.program_id(1) == 0)
    def _():
        h_ref[...] = _rms(x_ref[...], g_ref[...]).astype(BF16)

    o_ref[...] = jnp.dot(h_ref[...], w_ref[...], preferred_element_type=F32)


def _inproj(x, g, w, *, tm, tn):
    n, d = x.shape
    dp = w.shape[1]
    return pl.pallas_call(
        _inproj_kernel,
        grid=(n // tm, dp // tn),
        in_specs=[
            pl.BlockSpec((tm, d), lambda i, j: (i, 0)),
            pl.BlockSpec((1, d), lambda i, j: (0, 0)),
            pl.BlockSpec((d, tn), lambda i, j: (0, j)),
        ],
        out_specs=pl.BlockSpec((tm, tn), lambda i, j: (i, j)),
        out_shape=jax.ShapeDtypeStruct((n, dp), F32),
        scratch_shapes=[pltpu.VMEM((tm, d), BF16)],
        compiler_params=_params("parallel", "arbitrary"),
        name="inproj",
    )(x, g, w)


DFT_ROWS_PER_ANGLE_BLOCK = 64


def _dft_kernel(c_ref, s_ref, ac_ref, as_ref, bc_ref, bs_ref, *, seq, rows):
    nb = DFT_ROWS_PER_ANGLE_BLOCK
    period = 2 * seq

    @pl.when(pl.program_id(0) == 0)
    def _():
        t = lax.broadcasted_iota(jnp.int32, (nb, seq), 1)
        r = lax.broadcasted_iota(jnp.int32, (nb, seq), 0)
        step = math.pi / seq
        hi = (((r * nb) * t) & (period - 1)).astype(F32) * step
        lo = ((r * t) & (period - 1)).astype(F32) * step
        ac_ref[...] = jnp.cos(hi)
        as_ref[...] = jnp.sin(hi)
        bc_ref[...] = jnp.cos(lo)
        bs_ref[...] = jnp.sin(lo)

    bc = bc_ref[...]
    bs = bs_ref[...]
    for q in range(rows // nb):
        kh = pl.program_id(0) * (rows // nb) + q
        ac = ac_ref[pl.ds(kh, 1), :]
        sa = as_ref[pl.ds(kh, 1), :]
        c_ref[q * nb:(q + 1) * nb, :] = (ac * bc - sa * bs).astype(BF16)
        s_ref[q * nb:(q + 1) * nb, :] = (sa * bc + ac * bs).astype(BF16)


def _dft_tables(seq, *, rows):
    nb = DFT_ROWS_PER_ANGLE_BLOCK
    assert seq % rows == 0 and rows % nb == 0 and seq // nb <= nb and seq & (seq - 1) == 0
    return pl.pallas_call(
        functools.partial(_dft_kernel, seq=seq, rows=rows),
        grid=(seq // rows,),
        out_specs=[pl.BlockSpec((rows, seq), lambda i: (i, 0))] * 2,
        out_shape=[jax.ShapeDtypeStruct((seq, seq), BF16)] * 2,
        scratch_shapes=[pltpu.VMEM((nb, seq), F32)] * 4,
        compiler_params=_params("arbitrary"),
        name="dft_tables",
    )()


def _filter_kernel(z_ref, w1_ref, b1_ref, w2_ref, b2_ref, w3_ref, b3_ref, fr_ref, wo_ref, dl_ref,
                   kp_ref, km_ref, ny_ref, *, seq, rows):
    i = pl.program_id(0)
    hp = lax.Precision.HIGHEST
    fr = fr_ref[...]
    h = jnp.sin(fr * (jnp.dot(z_ref[...], w1_ref[...], precision=hp, preferred_element_type=F32) + b1_ref[...]))
    h = jnp.sin(fr * (jnp.dot(h, w2_ref[...], precision=hp, preferred_element_type=F32) + b2_ref[...]))
    h = jnp.sin(fr * (jnp.dot(h, w3_ref[...], precision=hp, preferred_element_type=F32) + b3_ref[...]))
    h = jnp.dot(h, wo_ref[...], precision=hp, preferred_element_type=F32)
    n = lax.broadcasted_iota(jnp.int32, (rows, 1), 0) + i * rows
    t = n.astype(F32) / float(max(seq - 1, 1))
    window = jnp.exp(-t * dl_ref[...]) + HYENA_SHIFT
    hf = h[:, :D_HYENA] * window
    hb = jnp.where(n == 0, 0.0, h[:, D_HYENA:] * window)
    kp = hf + hb
    kp_ref[...] = kp.astype(BF16)
    km_ref[...] = (hf - hb).astype(BF16)
    sign = jnp.where((n & 1) == 0, 1.0, -1.0)

    @pl.when(i == 0)
    def _():
        ny_ref[...] = jnp.zeros_like(ny_ref)

    ny_ref[...] += jnp.sum(kp * sign, axis=0, keepdims=True)


def _hyena_filter(z, w1, b1, w2, b2, w3, b3, fr, wo, deltas, *, rows):
    seq, zw = z.shape
    full = lambda a: pl.BlockSpec(a.shape, lambda i: (0,) * a.ndim)
    return pl.pallas_call(
        functools.partial(_filter_kernel, seq=seq, rows=rows),
        grid=(seq // rows,),
        in_specs=[pl.BlockSpec((rows, zw), lambda i: (i, 0))] + [full(a) for a in (w1, b1, w2, b2, w3, b3, fr, wo, deltas)],
        out_specs=[
            pl.BlockSpec((rows, D_HYENA), lambda i: (i, 0)),
            pl.BlockSpec((rows, D_HYENA), lambda i: (i, 0)),
            pl.BlockSpec((1, D_HYENA), lambda i: (0, 0)),
        ],
        out_shape=[
            jax.ShapeDtypeStruct((seq, D_HYENA), BF16),
            jax.ShapeDtypeStruct((seq, D_HYENA), BF16),
            jax.ShapeDtypeStruct((1, D_HYENA), F32),
        ],
        compiler_params=_params("arbitrary"),
        name="hyena_filter",
    )(z, w1, b1, w2, b2, w3, b3, fr, wo, deltas)


def _filter_dft_kernel(c_ref, s_ref, kp_ref, km_ref, ak_ref, bk_ref, *, seq, rows):
    k = lax.broadcasted_iota(jnp.int32, (rows, 1), 0) + pl.program_id(0) * rows
    w = jnp.where(k == 0, 0.5 / seq, 1.0 / seq)
    ak_ref[...] = w * jnp.dot(c_ref[...], kp_ref[...], preferred_element_type=F32)
    bk_ref[...] = w * jnp.dot(s_ref[...], km_ref[...], preferred_element_type=F32)


def _filter_dft(c, s, kp, km, *, rows):
    seq = c.shape[0]
    return pl.pallas_call(
        functools.partial(_filter_dft_kernel, seq=seq, rows=rows),
        grid=(seq // rows,),
        in_specs=[
            pl.BlockSpec((rows, seq), lambda i: (i, 0)),
            pl.BlockSpec((rows, seq), lambda i: (i, 0)),
            pl.BlockSpec((seq, D_HYENA), lambda i: (0, 0)),
            pl.BlockSpec((seq, D_HYENA), lambda i: (0, 0)),
        ],
        out_specs=[pl.BlockSpec((rows, D_HYENA), lambda i: (i, 0))] * 2,
        out_shape=[jax.ShapeDtypeStruct((seq, D_HYENA), F32)] * 2,
        compiler_params=_params("parallel"),
        name="filter_dft",
    )(c, s, kp, km)


def _shift_rows(u, halo_prev, halo_next, first, last):
    rows = u.shape[0]
    r = lax.broadcasted_iota(jnp.int32, (rows, 1), 0)
    prev_row = jnp.where(first, 0.0, halo_prev)
    next_row = jnp.where(last, 0.0, halo_next)
    um = jnp.where(r == 0, prev_row, pltpu.roll(u, 1, axis=0))
    up = jnp.where(r == rows - 1, next_row, pltpu.roll(u, rows - 1, axis=0))
    return um, up


def _mixprep_kernel(p_ref, pp_ref, pn_ref, hw_ref, hb_ref, sw_ref, zv_ref, zb_ref, x0_ref, ysc_ref):
    first = pl.program_id(1) == 0
    last = pl.program_id(1) == pl.num_programs(1) - 1
    nh = 3 * D_HYENA
    p = p_ref[:, :nh]
    pm, pp = _shift_rows(p, pp_ref[SUBLANES - 1:SUBLANES, :nh], pn_ref[0:1, :nh], first, last)
    u = pm * hw_ref[0:1, :] + p * hw_ref[1:2, :] + pp * hw_ref[2:3, :] + hb_ref[...]
    zv = u[:, 2 * D_HYENA:] * u[:, D_HYENA:2 * D_HYENA]
    zv_ref[...] = zv
    zb_ref[...] = zv.astype(BF16)
    x0_ref[...] = u[:, :D_HYENA]
    ds = D_SCONV
    gb = p_ref[:, nh:nh + ds]
    ch = p_ref[:, nh + ds:nh + 2 * ds] * p_ref[:, nh + 2 * ds:nh + 3 * ds]
    ch_prev = pp_ref[SUBLANES - 1:SUBLANES, nh + ds:nh + 2 * ds] * pp_ref[SUBLANES - 1:SUBLANES, nh + 2 * ds:nh + 3 * ds]
    ch_next = pn_ref[0:1, nh + ds:nh + 2 * ds] * pn_ref[0:1, nh + 2 * ds:nh + 3 * ds]
    cm, cp = _shift_rows(ch, ch_prev, ch_next, first, last)
    ysc_ref[...] = gb * (cm * sw_ref[0:1, :] + ch * sw_ref[1:2, :] + cp * sw_ref[2:3, :])


def _mixprep(p, hw, hb, sw, *, batch, seq, tl):
    n = p.shape[0]
    nl = seq // tl
    width = 3 * D_HYENA + 3 * D_SCONV
    hb8 = tl // SUBLANES
    last8 = seq // SUBLANES - 1
    row = lambda b, l: (b * nl + l, 0)
    return pl.pallas_call(
        _mixprep_kernel,
        grid=(batch, nl),
        in_specs=[
            pl.BlockSpec((tl, width), row),
            pl.BlockSpec((SUBLANES, width), lambda b, l: (b * (last8 + 1) + jnp.maximum(l * hb8 - 1, 0), 0)),
            pl.BlockSpec((SUBLANES, width), lambda b, l: (b * (last8 + 1) + jnp.minimum((l + 1) * hb8, last8), 0)),
            pl.BlockSpec(hw.shape, lambda b, l: (0, 0)),
            pl.BlockSpec(hb.shape, lambda b, l: (0, 0)),
            pl.BlockSpec(sw.shape, lambda b, l: (0, 0)),
        ],
        out_specs=[pl.BlockSpec((tl, D_HYENA), row)] * 4,
        out_shape=[
            jax.ShapeDtypeStruct((n, D_HYENA), F32),
            jax.ShapeDtypeStruct((n, D_HYENA), BF16),
            jax.ShapeDtypeStruct((n, D_HYENA), F32),
            jax.ShapeDtypeStruct((n, D_SCONV), F32),
        ],
        compiler_params=_params("parallel", "parallel"),
        name="mixprep",
    )(p, p, p, hw, hb, sw)


def _hyena_fwd_kernel(c_ref, s_ref, zb_ref, ak_ref, bk_ref, ny_ref, yc_ref, ys_ref, yn_ref, *, seq):
    @pl.when(pl.program_id(1) == 0)
    def _():
        n = lax.broadcasted_iota(jnp.int32, (seq, 1), 0)
        sign = jnp.where((n & 1) == 0, 1.0, -1.0)
        xn = jnp.sum(zb_ref[...].astype(F32) * sign, axis=0, keepdims=True)
        yn_ref[...] = (xn * ny_ref[...] * (0.5 / seq)).reshape(yn_ref.shape)

    zb = zb_ref[...]
    a = jnp.dot(c_ref[...], zb, preferred_element_type=F32)
    b = jnp.dot(s_ref[...], zb, preferred_element_type=F32)
    ak = ak_ref[...]
    bk = bk_ref[...]
    yc_ref[...] = (a * ak - b * bk).astype(BF16)
    ys_ref[...] = (a * bk + b * ak).astype(BF16)


def _hyena_fwd(c, s, zb, ak, bk, ny, *, batch, rows):
    seq = c.shape[0]
    nk = seq // rows
    return pl.pallas_call(
        functools.partial(_hyena_fwd_kernel, seq=seq),
        grid=(batch, nk),
        in_specs=[
            pl.BlockSpec((rows, seq), lambda b, i: (i, 0)),
            pl.BlockSpec((rows, seq), lambda b, i: (i, 0)),
            pl.BlockSpec((seq, D_HYENA), lambda b, i: (b, 0)),
            pl.BlockSpec((rows, D_HYENA), lambda b, i: (i, 0)),
            pl.BlockSpec((rows, D_HYENA), lambda b, i: (i, 0)),
            pl.BlockSpec((1, D_HYENA), lambda b, i: (0, 0)),
        ],
        out_specs=[
            pl.BlockSpec((rows, D_HYENA), lambda b, i: (b * nk + i, 0)),
            pl.BlockSpec((rows, D_HYENA), lambda b, i: (b * nk + i, 0)),
            pl.BlockSpec((1, 1, D_HYENA), lambda b, i: (b, 0, 0)),
        ],
        out_shape=[
            jax.ShapeDtypeStruct((batch * seq, D_HYENA), BF16),
            jax.ShapeDtypeStruct((batch * seq, D_HYENA), BF16),
            jax.ShapeDtypeStruct((batch, 1, D_HYENA), F32),
        ],
        compiler_params=_params("parallel", "arbitrary"),
        name="hyena_fwd",
    )(c, s, zb, ak, bk, ny)


def _hyena_inv_kernel(c_ref, s_ref, yc_ref, ys_ref, yn_ref, zv_ref, x0_ref, bd_ref, o_ref, *, rows):
    t = lax.broadcasted_iota(jnp.int32, (rows, 1), 0) + pl.program_id(1) * rows
    sign = jnp.where((t & 1) == 0, 1.0, -1.0)
    y = jnp.dot(c_ref[...], yc_ref[...], preferred_element_type=F32)
    y += jnp.dot(s_ref[...], ys_ref[...], preferred_element_type=F32)
    y += sign * yn_ref[0]
    o_ref[...] = x0_ref[...] * (y + zv_ref[...] * bd_ref[...])


def _hyena_inv(c, s, yc, ys, yn, zv, x0, bd, *, batch, rows):
    seq = c.shape[0]
    nt = seq // rows
    row = lambda b, i: (b * nt + i, 0)
    return pl.pallas_call(
        functools.partial(_hyena_inv_kernel, rows=rows),
        grid=(batch, nt),
        in_specs=[
            pl.BlockSpec((rows, seq), lambda b, i: (i, 0)),
            pl.BlockSpec((rows, seq), lambda b, i: (i, 0)),
            pl.BlockSpec((seq, D_HYENA), lambda b, i: (b, 0)),
            pl.BlockSpec((seq, D_HYENA), lambda b, i: (b, 0)),
            pl.BlockSpec((1, 1, D_HYENA), lambda b, i: (b, 0, 0)),
            pl.BlockSpec((rows, D_HYENA), row),
            pl.BlockSpec((rows, D_HYENA), row),
            pl.BlockSpec((1, D_HYENA), lambda b, i: (0, 0)),
        ],
        out_specs=pl.BlockSpec((rows, D_HYENA), row),
        out_shape=jax.ShapeDtypeStruct((batch * seq, D_HYENA), F32),
        compiler_params=_params("parallel", "parallel"),
        name="hyena_inv",
    )(c, s, yc, ys, yn, zv, x0, bd)


def _attn_kernel(q_ref, kp_ref, kc_ref, kn_ref, vp_ref, vc_ref, vn_ref, qg_ref, kg_ref, sink_ref, o_ref, bias_ref):
    nblk = pl.program_id(1)
    keys = 3 * BLOCK

    @pl.when((pl.program_id(0) == 0) & (nblk == 0))
    def _():
        qi = lax.broadcasted_iota(jnp.int32, (BLOCK, keys), 0)
        kj = lax.broadcasted_iota(jnp.int32, (BLOCK, keys), 1) - BLOCK
        dist = jnp.abs(qi - kj)
        distf = dist.astype(F32)
        for h in range(N_HEADS):
            slope = 2.0 ** (-8.0 * (h + 1) / N_HEADS)
            bias_ref[h] = jnp.where(dist <= WINDOW, -slope * distf, NEG_INF)

    kj = lax.broadcasted_iota(jnp.int32, (1, keys), 1)
    lo = jnp.where(nblk == 0, BLOCK, 0)
    hi = jnp.where(nblk == pl.num_programs(1) - 1, 2 * BLOCK, keys)
    edge = jnp.where((kj >= lo) & (kj < hi), 0.0, NEG_INF)

    seg_r = lax.broadcasted_iota(jnp.int32, (D_KV, D_KV), 0) // HEAD_DIM
    seg_c = lax.broadcasted_iota(jnp.int32, (D_KV, D_KV), 1) // HEAD_DIM
    seg_mean = jnp.where(seg_r == seg_c, 1.0 / HEAD_DIM, 0.0).astype(BF16)

    def head_norm(x, g):
        ms = jnp.dot((x * x).astype(BF16), seg_mean, preferred_element_type=F32)
        return x * lax.rsqrt(ms + EPS) * g

    qg = jnp.concatenate([qg_ref[...]] * GROUP, axis=1) * (HEAD_DIM ** -0.5)
    kg = jnp.concatenate([kg_ref[...]] * N_KV_HEADS, axis=1)
    k = jnp.concatenate([kp_ref[...], kc_ref[...], kn_ref[...]], axis=0)
    kn = head_norm(k, kg).astype(BF16)
    v = jnp.concatenate([vp_ref[...], vc_ref[...], vn_ref[...]], axis=0).astype(BF16)

    for kv in range(N_KV_HEADS):
        qn = head_norm(q_ref[:, kv * D_KV:(kv + 1) * D_KV], qg).astype(BF16)
        kh = kn[:, kv * HEAD_DIM:(kv + 1) * HEAD_DIM]
        vh = v[:, kv * HEAD_DIM:(kv + 1) * HEAD_DIM]
        for g in range(GROUP):
            h = kv * GROUP + g
            qh = qn[:, g * HEAD_DIM:(g + 1) * HEAD_DIM]
            s = lax.dot_general(qh, kh, (((1,), (1,)), ((), ())), preferred_element_type=F32)
            logits = s + bias_ref[h] + edge
            sink = sink_ref[0:1, h:h + 1]
            m = jnp.maximum(jnp.max(logits, axis=-1, keepdims=True), sink)
            p = jnp.exp(logits - m)
            denom = jnp.sum(p, axis=-1, keepdims=True) + jnp.exp(sink - m)
            o = jnp.dot(p.astype(BF16), vh, preferred_element_type=F32)
            o_ref[:, h * HEAD_DIM:(h + 1) * HEAD_DIM] = o / denom


def _attention(p, qg, kg, sink, *, batch, seq):
    n = p.shape[0]
    nb = seq // BLOCK
    q_col = (3 * D_HYENA + 3 * D_SCONV) // D_ATTN
    k_col = (3 * D_HYENA + 3 * D_SCONV + D_ATTN) // D_KV
    v_col = k_col + 1
    assert q_col * D_ATTN == 3 * D_HYENA + 3 * D_SCONV and k_col * D_KV == q_col * D_ATTN + D_ATTN

    def kv_spec(col, off):
        def idx(b, i):
            return (b * nb + jnp.clip(i + off, 0, nb - 1), col)
        return pl.BlockSpec((BLOCK, D_KV), idx)

    return pl.pallas_call(
        _attn_kernel,
        grid=(batch, nb),
        in_specs=[
            pl.BlockSpec((BLOCK, D_ATTN), lambda b, i: (b * nb + i, q_col)),
            kv_spec(k_col, -1), kv_spec(k_col, 0), kv_spec(k_col, 1),
            kv_spec(v_col, -1), kv_spec(v_col, 0), kv_spec(v_col, 1),
            pl.BlockSpec((1, HEAD_DIM), lambda b, i: (0, 0)),
            pl.BlockSpec((1, HEAD_DIM), lambda b, i: (0, 0)),
            pl.BlockSpec((1, N_HEADS), lambda b, i: (0, 0)),
        ],
        out_specs=pl.BlockSpec((BLOCK, D_ATTN), lambda b, i: (b * nb + i, 0)),
        out_shape=jax.ShapeDtypeStruct((n, D_ATTN), F32),
        scratch_shapes=[pltpu.VMEM((N_HEADS, BLOCK, 3 * BLOCK), F32)],
        compiler_params=_params("arbitrary", "arbitrary"),
        name="attention",
    )(p, p, p, p, p, p, p, qg, kg, sink)


def _outproj_kernel(x_ref, yh_ref, ys_ref, ya_ref, g_ref, w_ref, o_ref):
    d1 = D_HYENA
    d2 = D_HYENA + D_SCONV
    nh = _rms(yh_ref[...], g_ref[:, :d1]).astype(BF16)
    ns = _rms(ys_ref[...], g_ref[:, d1:d2]).astype(BF16)
    na = _rms(ya_ref[...], g_ref[:, d2:]).astype(BF16)
    acc = jnp.dot(nh, w_ref[:d1, :], preferred_element_type=F32)
    acc += jnp.dot(ns, w_ref[d1:d2, :], preferred_element_type=F32)
    acc += jnp.dot(na, w_ref[d2:, :], preferred_element_type=F32)
    o_ref[...] = x_ref[...] + acc


def _outproj(x, yh, ys, ya, g, w, *, tm):
    n, d = x.shape
    row = lambda i: (i, 0)
    return pl.pallas_call(
        _outproj_kernel,
        grid=(n // tm,),
        in_specs=[
            pl.BlockSpec((tm, d), row),
            pl.BlockSpec((tm, D_HYENA), row),
            pl.BlockSpec((tm, D_SCONV), row),
            pl.BlockSpec((tm, D_ATTN), row),
            pl.BlockSpec((1, w.shape[0]), lambda i: (0, 0)),
            pl.BlockSpec(w.shape, lambda i: (0, 0)),
        ],
        out_specs=pl.BlockSpec((tm, d), row),
        out_shape=jax.ShapeDtypeStruct((n, d), F32),
        compiler_params=_params("parallel"),
        name="outproj",
    )(x, yh, ys, ya, g, w)


def _tile(n, want):
    t = min(n, want)
    assert n % t == 0
    return t


def _filter_features(seq):
    n = jnp.arange(seq, dtype=F32)
    t = n / float(max(seq - 1, 1))
    bands = jnp.linspace(1e-4, FILTER_BANDS - 1, FILTER_BANDS, dtype=F32)
    ang = (2.0 * math.pi / seq) * n[:, None] * bands[None, :]
    z = jnp.concatenate([t[:, None], jnp.cos(ang), jnp.sin(ang)], axis=-1)
    return jnp.pad(z, ((0, 0), (0, 128 - z.shape[1])))


def kernel(x, norm_ffn1, ffn1_w_gate, ffn1_w_up, ffn1_w_down, norm_mix, w_in, hyena_short_w, hyena_short_b,
           filt_w1, filt_b1, filt_w2, filt_b2, filt_w3, filt_b3, filt_freq, filt_w_out, hyena_bias, sconv_w,
           q_norm_g, k_norm_g, attn_sink, mix_out_g, w_out, norm_ffn2, ffn2_w_gate, ffn2_w_up, ffn2_w_down):
    batch, seq, d = x.shape
    depth = w_in.shape[0]
    n = batch * seq
    tm = _tile(n, 1024)
    tf = _tile(ffn1_w_gate.shape[2], 512)
    tn = _tile(w_in.shape[2], 1536)
    tl = _tile(seq, 512)
    tk = _tile(seq, 512)

    xf = x.reshape(n, d)
    row = lambda a: a.reshape(1, -1)
    c_tab, s_tab = _dft_tables(seq, rows=_tile(seq, 256))
    z = _filter_features(seq)
    deltas = jnp.abs(jnp.linspace(HYENA_MIN_DECAY, HYENA_MAX_DECAY, D_HYENA, dtype=F32)).reshape(1, -1)

    for l in range(depth):
        xf = _ffn(xf, row(norm_ffn1[l]), ffn1_w_gate[l].astype(BF16), ffn1_w_up[l].astype(BF16),
                  ffn1_w_down[l].astype(BF16), tm=tm, tf=tf)

        p = _inproj(xf, row(norm_mix[l]), w_in[l].astype(BF16), tm=tm, tn=tn)

        w1 = jnp.pad(filt_w1[l], ((0, z.shape[1] - filt_w1.shape[1]), (0, 0)))
        kp, km, ny = _hyena_filter(z, w1, row(filt_b1[l]), filt_w2[l], row(filt_b2[l]), filt_w3[l], row(filt_b3[l]),
                                   row(filt_freq[l]), filt_w_out[l], deltas, rows=tk)
        ak, bk = _filter_dft(c_tab, s_tab, kp, km, rows=tk)

        zv, zb, x0, y_sc = _mixprep(p, hyena_short_w[l], row(hyena_short_b[l]), sconv_w[l], batch=batch, seq=seq, tl=tl)
        yc, ys, yn = _hyena_fwd(c_tab, s_tab, zb, ak, bk, ny, batch=batch, rows=tk)
        y_hy = _hyena_inv(c_tab, s_tab, yc, ys, yn, zv, x0, row(hyena_bias[l]), batch=batch, rows=tk)

        y_at = _attention(p, row(q_norm_g[l]), row(k_norm_g[l]), row(attn_sink[l]), batch=batch, seq=seq)

        xf = _outproj(xf, y_hy, y_sc, y_at, row(mix_out_g[l]), w_out[l].astype(BF16), tm=_tile(n, 512))

        xf = _ffn(xf, row(norm_ffn2[l]), ffn2_w_gate[l].astype(BF16), ffn2_w_up[l].astype(BF16),
                  ffn2_w_down[l].astype(BF16), tm=tm, tf=tf)

    return xf.reshape(batch, seq, d)
```

```python
import functools
import math

import jax
import jax.numpy as jnp
from jax import lax
from jax.experimental import pallas as pl
from jax.experimental.pallas import tpu as pltpu

F32 = jnp.float32
BF16 = jnp.bfloat16

EPS = 1e-6
NEG_INF = -1e30

D_HYENA = 512
D_SCONV = 512
N_HEADS = 16
N_KV_HEADS = 4
GROUP = N_HEADS // N_KV_HEADS
HEAD_DIM = 64
D_ATTN = N_HEADS * HEAD_DIM
D_KV = N_KV_HEADS * HEAD_DIM
WINDOW = 128
BLOCK = 128
FILTER_BANDS = 16
FILTER_HIDDEN = 64
HYENA_MIN_DECAY = math.log(1e-2) / 1.5
HYENA_MAX_DECAY = math.log(1e-2) / 0.3
HYENA_SHIFT = 0.05

V7X_VMEM_LIMIT_BYTES = 60 * 1024 * 1024
HALO_ROWS = 16
LANES = 128


def _params(*sem):
    return pltpu.CompilerParams(dimension_semantics=sem, vmem_limit_bytes=V7X_VMEM_LIMIT_BYTES)


def _rms(x, g):
    ms = jnp.mean(x * x, axis=-1, keepdims=True)
    return x * lax.rsqrt(ms + EPS) * g


CAST_BLOCK_BYTES = 6 * 1024 * 1024


def _cast_kernel(w_ref, o_ref):
    o_ref[...] = w_ref[...].astype(BF16)


def _cast_bf16(w):
    depth, r, c = w.shape
    rows = r
    while rows * c * 4 > CAST_BLOCK_BYTES and rows % 32 == 0:
        rows //= 2
    spec = pl.BlockSpec((None, rows, c), lambda l, i: (l, i, 0))
    return pl.pallas_call(
        _cast_kernel,
        grid=(depth, r // rows),
        in_specs=[spec],
        out_specs=spec,
        out_shape=jax.ShapeDtypeStruct(w.shape, BF16),
        compiler_params=_params("parallel", "parallel"),
        name="cast_bf16",
    )(w)


def _ffn_kernel(x_ref, g_ref, wg_ref, wu_ref, wd_ref, o_ref, h_ref):
    @pl.when(pl.program_id(1) == 0)
    def _():
        x = x_ref[...]
        h_ref[...] = _rms(x, g_ref[...]).astype(BF16)
        o_ref[...] = x

    h = h_ref[...]
    gate = jnp.dot(h, wg_ref[...], preferred_element_type=F32)
    up = jnp.dot(h, wu_ref[...], preferred_element_type=F32)
    a = (gate * jax.nn.sigmoid(gate) * up * 0.5).astype(BF16)
    o_ref[...] += jnp.dot(a, wd_ref[...], preferred_element_type=F32)


def _ffn(x, g, wg, wu, wd, *, layer, tm, tf):
    n, d = x.shape
    ff = wg.shape[2]
    return pl.pallas_call(
        _ffn_kernel,
        grid=(n // tm, ff // tf),
        in_specs=[
            pl.BlockSpec((tm, d), lambda i, j: (i, 0)),
            pl.BlockSpec((1, d), lambda i, j: (0, 0)),
            pl.BlockSpec((None, d, tf), lambda i, j: (layer, 0, j)),
            pl.BlockSpec((None, d, tf), lambda i, j: (layer, 0, j)),
            pl.BlockSpec((None, tf, d), lambda i, j: (layer, j, 0)),
        ],
        out_specs=pl.BlockSpec((tm, d), lambda i, j: (i, 0)),
        out_shape=jax.ShapeDtypeStruct((n, d), F32),
        scratch_shapes=[pltpu.VMEM((tm, d), BF16)],
        compiler_params=_params("parallel", "arbitrary"),
        name="ffn",
    )(x, g, wg, wu, wd)


def _inproj_kernel(x_ref, g_ref, w_ref, o_ref, h_ref):
    @pl.when(pl.program_id(1) == 0)
    def _():
        h_ref[...] = _rms(x_ref[...], g_ref[...]).astype(BF16)

    o_ref[...] = jnp.dot(h_ref[...], w_ref[...], preferred_element_type=F32).astype(o_ref.dtype)


def _inproj(x, g, w, *, layer, tm, tn):
    n, d = x.shape
    dp = w.shape[2]
    return pl.pallas_call(
        _inproj_kernel,
        grid=(n // tm, dp // tn),
        in_specs=[
            pl.BlockSpec((tm, d), lambda i, j: (i, 0)),
            pl.BlockSpec((1, d), lambda i, j: (0, 0)),
            pl.BlockSpec((None, d, tn), lambda i, j: (layer, 0, j)),
        ],
        out_specs=pl.BlockSpec((tm, tn), lambda i, j: (i, j)),
        out_shape=jax.ShapeDtypeStruct((n, dp), BF16),
        scratch_shapes=[pltpu.VMEM((tm, d), BF16)],
        compiler_params=_params("parallel", "arbitrary"),
        name="inproj",
    )(x, g, w)


DFT_ROWS_PER_ANGLE_BLOCK = 64


def _dft_kernel(c_ref, s_ref, ac_ref, as_ref, bc_ref, bs_ref, *, seq, rows):
    nb = DFT_ROWS_PER_ANGLE_BLOCK
    period = 2 * seq

    @pl.when(pl.program_id(0) == 0)
    def _():
        t = lax.broadcasted_iota(jnp.int32, (nb, seq), 1)
        r = lax.broadcasted_iota(jnp.int32, (nb, seq), 0)
        step = math.pi / seq
        hi = (((r * nb) * t) & (period - 1)).astype(F32) * step
        lo = ((r * t) & (period - 1)).astype(F32) * step
        ac_ref[...] = jnp.cos(hi)
        as_ref[...] = jnp.sin(hi)
        bc_ref[...] = jnp.cos(lo)
        bs_ref[...] = jnp.sin(lo)

    bc = bc_ref[...]
    bs = bs_ref[...]
    for q in range(rows // nb):
        kh = pl.program_id(0) * (rows // nb) + q
        ac = ac_ref[pl.ds(kh, 1), :]
        sa = as_ref[pl.ds(kh, 1), :]
        c_ref[q * nb:(q + 1) * nb, :] = (ac * bc - sa * bs).astype(BF16)
        s_ref[q * nb:(q + 1) * nb, :] = (sa * bc + ac * bs).astype(BF16)


def _dft_tables(seq, *, rows):
    nb = DFT_ROWS_PER_ANGLE_BLOCK
    assert seq % rows == 0 and rows % nb == 0 and seq // nb <= nb and seq & (seq - 1) == 0
    return pl.pallas_call(
        functools.partial(_dft_kernel, seq=seq, rows=rows),
        grid=(seq // rows,),
        out_specs=[pl.BlockSpec((rows, seq), lambda i: (i, 0))] * 2,
        out_shape=[jax.ShapeDtypeStruct((seq, seq), BF16)] * 2,
        scratch_shapes=[pltpu.VMEM((nb, seq), F32)] * 4,
        compiler_params=_params("arbitrary"),
        name="dft_tables",
    )()


def _filter_kernel(z_ref, w1_ref, b1_ref, w2_ref, b2_ref, w3_ref, b3_ref, fr_ref, wo_ref, dl_ref,
                   kp_ref, km_ref, ny_ref, *, seq, rows):
    i = pl.program_id(0)
    hp = lax.Precision.HIGHEST
    fr = fr_ref[...]
    h = jnp.sin(fr * (jnp.dot(z_ref[...], w1_ref[...], precision=hp, preferred_element_type=F32) + b1_ref[...]))
    h = jnp.sin(fr * (jnp.dot(h, w2_ref[...], precision=hp, preferred_element_type=F32) + b2_ref[...]))
    h = jnp.sin(fr * (jnp.dot(h, w3_ref[...], precision=hp, preferred_element_type=F32) + b3_ref[...]))
    h = jnp.dot(h, wo_ref[...], precision=hp, preferred_element_type=F32)
    n = lax.broadcasted_iota(jnp.int32, (rows, 1), 0) + i * rows
    t = n.astype(F32) / float(max(seq - 1, 1))
    window = jnp.exp(-t * dl_ref[...]) + HYENA_SHIFT
    hf = h[:, :D_HYENA] * window
    hb = jnp.where(n == 0, 0.0, h[:, D_HYENA:] * window)
    kp = hf + hb
    kp_ref[...] = kp.astype(BF16)
    km_ref[...] = (hf - hb).astype(BF16)
    sign = jnp.where((n & 1) == 0, 1.0, -1.0)

    @pl.when(i == 0)
    def _():
        ny_ref[...] = jnp.zeros_like(ny_ref)

    ny_ref[...] += jnp.sum(kp * sign, axis=0, keepdims=True)


def _hyena_filter(z, w1, b1, w2, b2, w3, b3, fr, wo, deltas, *, rows):
    seq, zw = z.shape
    full = lambda a: pl.BlockSpec(a.shape, lambda i: (0,) * a.ndim)
    return pl.pallas_call(
        functools.partial(_filter_kernel, seq=seq, rows=rows),
        grid=(seq // rows,),
        in_specs=[pl.BlockSpec((rows, zw), lambda i: (i, 0))] + [full(a) for a in (w1, b1, w2, b2, w3, b3, fr, wo, deltas)],
        out_specs=[
            pl.BlockSpec((rows, D_HYENA), lambda i: (i, 0)),
            pl.BlockSpec((rows, D_HYENA), lambda i: (i, 0)),
            pl.BlockSpec((1, D_HYENA), lambda i: (0, 0)),
        ],
        out_shape=[
            jax.ShapeDtypeStruct((seq, D_HYENA), BF16),
            jax.ShapeDtypeStruct((seq, D_HYENA), BF16),
            jax.ShapeDtypeStruct((1, D_HYENA), F32),
        ],
        compiler_params=_params("arbitrary"),
        name="hyena_filter",
    )(z, w1, b1, w2, b2, w3, b3, fr, wo, deltas)


def _filter_dft_kernel(c_ref, s_ref, kp_ref, km_ref, ak_ref, bk_ref, *, seq, rows):
    k = lax.broadcasted_iota(jnp.int32, (rows, 1), 0) + pl.program_id(0) * rows
    w = jnp.where(k == 0, 0.5 / seq, 1.0 / seq)
    ak_ref[...] = w * jnp.dot(c_ref[...], kp_ref[...], preferred_element_type=F32)
    bk_ref[...] = w * jnp.dot(s_ref[...], km_ref[...], preferred_element_type=F32)


def _filter_dft(c, s, kp, km, *, rows):
    seq = c.shape[0]
    return pl.pallas_call(
        functools.partial(_filter_dft_kernel, seq=seq, rows=rows),
        grid=(seq // rows,),
        in_specs=[
            pl.BlockSpec((rows, seq), lambda i: (i, 0)),
            pl.BlockSpec((rows, seq), lambda i: (i, 0)),
            pl.BlockSpec((seq, D_HYENA), lambda i: (0, 0)),
            pl.BlockSpec((seq, D_HYENA), lambda i: (0, 0)),
        ],
        out_specs=[pl.BlockSpec((rows, D_HYENA), lambda i: (i, 0))] * 2,
        out_shape=[jax.ShapeDtypeStruct((seq, D_HYENA), F32)] * 2,
        compiler_params=_params("parallel"),
        name="filter_dft",
    )(c, s, kp, km)


def _shift_rows(u, halo_prev, halo_next, first, last):
    rows = u.shape[0]
    r = lax.broadcasted_iota(jnp.int32, (rows, 1), 0)
    prev_row = jnp.where(first, 0.0, halo_prev)
    next_row = jnp.where(last, 0.0, halo_next)
    um = jnp.where(r == 0, prev_row, pltpu.roll(u, 1, axis=0))
    up = jnp.where(r == rows - 1, next_row, pltpu.roll(u, rows - 1, axis=0))
    return um, up


def _mixprep_kernel(p_ref, pp_ref, pn_ref, hw_ref, hb_ref, sw_ref, zv_ref, zb_ref, x0_ref, ysc_ref):
    first = pl.program_id(1) == 0
    last = pl.program_id(1) == pl.num_programs(1) - 1
    nh = 3 * D_HYENA
    hr = HALO_ROWS
    p = p_ref[:, :nh].astype(F32)
    pm, pp = _shift_rows(p, pp_ref[hr - 1:hr, :nh].astype(F32), pn_ref[0:1, :nh].astype(F32), first, last)
    u = pm * hw_ref[0:1, :] + p * hw_ref[1:2, :] + pp * hw_ref[2:3, :] + hb_ref[...]
    zv = u[:, 2 * D_HYENA:] * u[:, D_HYENA:2 * D_HYENA]
    zv_ref[...] = zv
    zb_ref[...] = zv.astype(BF16)
    x0_ref[...] = u[:, :D_HYENA]
    ds = D_SCONV
    gb = p_ref[:, nh:nh + ds].astype(F32)
    ch = p_ref[:, nh + ds:nh + 2 * ds].astype(F32) * p_ref[:, nh + 2 * ds:nh + 3 * ds].astype(F32)
    ch_prev = pp_ref[hr - 1:hr, nh + ds:nh + 2 * ds].astype(F32) * pp_ref[hr - 1:hr, nh + 2 * ds:nh + 3 * ds].astype(F32)
    ch_next = pn_ref[0:1, nh + ds:nh + 2 * ds].astype(F32) * pn_ref[0:1, nh + 2 * ds:nh + 3 * ds].astype(F32)
    cm, cp = _shift_rows(ch, ch_prev, ch_next, first, last)
    ysc_ref[...] = gb * (cm * sw_ref[0:1, :] + ch * sw_ref[1:2, :] + cp * sw_ref[2:3, :])


def _mixprep(p, hw, hb, sw, *, batch, seq, tl):
    n = p.shape[0]
    nl = seq // tl
    width = 3 * D_HYENA + 3 * D_SCONV
    hb8 = tl // HALO_ROWS
    last8 = seq // HALO_ROWS - 1
    row = lambda b, l: (b * nl + l, 0)
    return pl.pallas_call(
        _mixprep_kernel,
        grid=(batch, nl),
        in_specs=[
            pl.BlockSpec((tl, width), row),
            pl.BlockSpec((HALO_ROWS, width), lambda b, l: (b * (last8 + 1) + jnp.maximum(l * hb8 - 1, 0), 0)),
            pl.BlockSpec((HALO_ROWS, width), lambda b, l: (b * (last8 + 1) + jnp.minimum((l + 1) * hb8, last8), 0)),
            pl.BlockSpec(hw.shape, lambda b, l: (0, 0)),
            pl.BlockSpec(hb.shape, lambda b, l: (0, 0)),
            pl.BlockSpec(sw.shape, lambda b, l: (0, 0)),
        ],
        out_specs=[pl.BlockSpec((tl, D_HYENA), row)] * 4,
        out_shape=[
            jax.ShapeDtypeStruct((n, D_HYENA), F32),
            jax.ShapeDtypeStruct((n, D_HYENA), BF16),
            jax.ShapeDtypeStruct((n, D_HYENA), F32),
            jax.ShapeDtypeStruct((n, D_SCONV), F32),
        ],
        compiler_params=_params("parallel", "parallel"),
        name="mixprep",
    )(p, p, p, hw, hb, sw)


def _hyena_fwd_kernel(c_ref, s_ref, zb_ref, ak_ref, bk_ref, ny_ref, yc_ref, ys_ref, yn_ref, *, seq):
    @pl.when(pl.program_id(1) == 0)
    def _():
        n = lax.broadcasted_iota(jnp.int32, (seq, 1), 0)
        sign = jnp.where((n & 1) == 0, 1.0, -1.0)
        xn = jnp.sum(zb_ref[...].astype(F32) * sign, axis=0, keepdims=True)
        yn_ref[...] = (xn * ny_ref[...] * (0.5 / seq)).reshape(yn_ref.shape)

    zb = zb_ref[...]
    a = jnp.dot(c_ref[...], zb, preferred_element_type=F32)
    b = jnp.dot(s_ref[...], zb, preferred_element_type=F32)
    ak = ak_ref[...]
    bk = bk_ref[...]
    yc_ref[...] = (a * ak - b * bk).astype(BF16)
    ys_ref[...] = (a * bk + b * ak).astype(BF16)


def _hyena_fwd(c, s, zb, ak, bk, ny, *, batch, rows):
    seq = c.shape[0]
    nk = seq // rows
    return pl.pallas_call(
        functools.partial(_hyena_fwd_kernel, seq=seq),
        grid=(batch, nk),
        in_specs=[
            pl.BlockSpec((rows, seq), lambda b, i: (i, 0)),
            pl.BlockSpec((rows, seq), lambda b, i: (i, 0)),
            pl.BlockSpec((seq, D_HYENA), lambda b, i: (b, 0)),
            pl.BlockSpec((rows, D_HYENA), lambda b, i: (i, 0)),
            pl.BlockSpec((rows, D_HYENA), lambda b, i: (i, 0)),
            pl.BlockSpec((1, D_HYENA), lambda b, i: (0, 0)),
        ],
        out_specs=[
            pl.BlockSpec((rows, D_HYENA), lambda b, i: (b * nk + i, 0)),
            pl.BlockSpec((rows, D_HYENA), lambda b, i: (b * nk + i, 0)),
            pl.BlockSpec((1, 1, D_HYENA), lambda b, i: (b, 0, 0)),
        ],
        out_shape=[
            jax.ShapeDtypeStruct((batch * seq, D_HYENA), BF16),
            jax.ShapeDtypeStruct((batch * seq, D_HYENA), BF16),
            jax.ShapeDtypeStruct((batch, 1, D_HYENA), F32),
        ],
        compiler_params=_params("parallel", "arbitrary"),
        name="hyena_fwd",
    )(c, s, zb, ak, bk, ny)


def _hyena_inv_kernel(c_ref, s_ref, yc_ref, ys_ref, yn_ref, zv_ref, x0_ref, bd_ref, o_ref, *, rows):
    t = lax.broadcasted_iota(jnp.int32, (rows, 1), 0) + pl.program_id(1) * rows
    sign = jnp.where((t & 1) == 0, 1.0, -1.0)
    y = jnp.dot(c_ref[...], yc_ref[...], preferred_element_type=F32)
    y += jnp.dot(s_ref[...], ys_ref[...], preferred_element_type=F32)
    y += sign * yn_ref[0]
    o_ref[...] = x0_ref[...] * (y + zv_ref[...] * bd_ref[...])


def _hyena_inv(c, s, yc, ys, yn, zv, x0, bd, *, batch, rows):
    seq = c.shape[0]
    nt = seq // rows
    row = lambda b, i: (b * nt + i, 0)
    return pl.pallas_call(
        functools.partial(_hyena_inv_kernel, rows=rows),
        grid=(batch, nt),
        in_specs=[
            pl.BlockSpec((rows, seq), lambda b, i: (i, 0)),
            pl.BlockSpec((rows, seq), lambda b, i: (i, 0)),
            pl.BlockSpec((seq, D_HYENA), lambda b, i: (b, 0)),
            pl.BlockSpec((seq, D_HYENA), lambda b, i: (b, 0)),
            pl.BlockSpec((1, 1, D_HYENA), lambda b, i: (b, 0, 0)),
            pl.BlockSpec((rows, D_HYENA), row),
            pl.BlockSpec((rows, D_HYENA), row),
            pl.BlockSpec((1, D_HYENA), lambda b, i: (0, 0)),
        ],
        out_specs=pl.BlockSpec((rows, D_HYENA), row),
        out_shape=jax.ShapeDtypeStruct((batch * seq, D_HYENA), F32),
        compiler_params=_params("parallel", "parallel"),
        name="hyena_inv",
    )(c, s, yc, ys, yn, zv, x0, bd)


HEADS_PER_TILE = LANES // HEAD_DIM
ATTN_GROUPS = tuple(
    tuple(h for h in range(N_HEADS)
          if (h // GROUP) // HEADS_PER_TILE == tile and (h % HEADS_PER_TILE != (h // GROUP) % HEADS_PER_TILE) == swapped)
    for tile in range(N_KV_HEADS // HEADS_PER_TILE) for swapped in (False, True))
ATTN_GROUP_ROWS = len(ATTN_GROUPS[0]) * BLOCK


def _attn_kernel(q_ref, kp_ref, kc_ref, kn_ref, vp_ref, vc_ref, vn_ref, qg_ref, kg_ref, sink_ref, o_ref, bias_ref,
                 *, blocks):
    step = pl.program_id(1)
    last_step = pl.num_programs(1) - 1
    keys = 3 * BLOCK
    grows = ATTN_GROUP_ROWS

    @pl.when((pl.program_id(0) == 0) & (step == 0))
    def _():
        qi = lax.broadcasted_iota(jnp.int32, (BLOCK, keys), 0)
        kj = lax.broadcasted_iota(jnp.int32, (BLOCK, keys), 1)
        dist = jnp.abs(qi - (kj - BLOCK))
        distf = dist.astype(F32)
        for variant, (lo, hi) in enumerate(((BLOCK, keys), (0, keys), (0, 2 * BLOCK))):
            ok = (dist <= WINDOW) & (kj >= lo) & (kj < hi)
            for g, heads in enumerate(ATTN_GROUPS):
                for j, h in enumerate(heads):
                    slope = 2.0 ** (-8.0 * (h + 1) / N_HEADS)
                    bias_ref[variant, g, j * BLOCK:(j + 1) * BLOCK, :] = jnp.where(ok, -slope * distf, NEG_INF)

    seg_r = lax.broadcasted_iota(jnp.int32, (D_KV, D_KV), 0) // HEAD_DIM
    seg_c = lax.broadcasted_iota(jnp.int32, (D_KV, D_KV), 1) // HEAD_DIM
    seg_mean = jnp.where(seg_r == seg_c, 1.0 / HEAD_DIM, 0.0).astype(BF16)

    def head_norm(x, g):
        ms = jnp.dot((x * x).astype(BF16), seg_mean, preferred_element_type=F32)
        return x * lax.rsqrt(ms + EPS) * g

    low = lax.broadcasted_iota(jnp.int32, (1, LANES), 1) < HEAD_DIM
    qg = jnp.concatenate([qg_ref[...]] * GROUP, axis=1) * (HEAD_DIM ** -0.5)
    kg = jnp.concatenate([kg_ref[...]] * N_KV_HEADS, axis=1)
    k = jnp.concatenate([kp_ref[...], kc_ref[...], kn_ref[...]], axis=0).astype(F32)
    k = head_norm(k, kg)
    v = jnp.concatenate([vp_ref[...], vc_ref[...], vn_ref[...]], axis=0).astype(F32)
    k_var, v_var = [], []
    for t in range(D_KV // LANES):
        kt = k[:, t * LANES:(t + 1) * LANES]
        vt = v[:, t * LANES:(t + 1) * LANES]
        k_var += [kt.astype(BF16), pltpu.roll(kt, HEAD_DIM, axis=1).astype(BF16)]
        v_var += [vt.astype(BF16), pltpu.roll(vt, HEAD_DIM, axis=1).astype(BF16)]

    q_half = {}
    for c in range(N_KV_HEADS):
        qn = head_norm(q_ref[:, c * D_KV:(c + 1) * D_KV].astype(F32), qg)
        for u in range(D_KV // LANES):
            qt = qn[:, u * LANES:(u + 1) * LANES]
            tile = c * (D_KV // LANES) + u
            q_half[tile, 0] = jnp.where(low, qt, 0.0).astype(BF16)
            q_half[tile, 1] = jnp.where(low, 0.0, qt).astype(BF16)

    row_head = lax.broadcasted_iota(jnp.int32, (grows, 1), 0) // BLOCK
    sinks = []
    for heads in ATTN_GROUPS:
        col = jnp.zeros((grows, 1), F32)
        for j, h in enumerate(heads):
            col = jnp.where(row_head == j, sink_ref[0:1, h:h + 1], col)
        sinks.append(col)

    for qb in range(blocks):
        rows = slice(qb * BLOCK, (qb + 1) * BLOCK)
        krows = slice(qb * BLOCK, qb * BLOCK + keys)
        if qb == 0:
            variant = jnp.where(step == 0, 0, 1)
        elif qb == blocks - 1:
            variant = jnp.where(step == last_step, 2, 1)
        else:
            variant = 1
        res = {}
        for g, heads in enumerate(ATTN_GROUPS):
            qs = jnp.concatenate([q_half[h // HEADS_PER_TILE, h % HEADS_PER_TILE][rows] for h in heads], axis=0)
            s = lax.dot_general(qs, k_var[g][krows], (((1,), (1,)), ((), ())), preferred_element_type=F32)
            logits = s + bias_ref[variant, g]
            m = jnp.maximum(jnp.max(logits, axis=-1, keepdims=True), sinks[g])
            p = jnp.exp(logits - m)
            denom = jnp.sum(p, axis=-1, keepdims=True) + jnp.exp(sinks[g] - m)
            o = jnp.dot(p.astype(BF16), v_var[g][krows], preferred_element_type=F32) / denom
            for j, h in enumerate(heads):
                res[h] = o[j * BLOCK:(j + 1) * BLOCK]
        for tile in range(N_HEADS // HEADS_PER_TILE):
            o_ref[rows, tile * LANES:(tile + 1) * LANES] = jnp.where(
                low, res[HEADS_PER_TILE * tile], res[HEADS_PER_TILE * tile + 1])


def _attention(p, qg, kg, sink, *, batch, seq, blocks):
    n = p.shape[0]
    nb = seq // BLOCK
    tq = blocks * BLOCK
    ns = seq // tq
    assert HEADS_PER_TILE == 2 and blocks >= 2 and seq % tq == 0
    q_col = (3 * D_HYENA + 3 * D_SCONV) // D_ATTN
    k_col = (3 * D_HYENA + 3 * D_SCONV + D_ATTN) // D_KV
    v_col = k_col + 1
    assert q_col * D_ATTN == 3 * D_HYENA + 3 * D_SCONV and k_col * D_KV == q_col * D_ATTN + D_ATTN

    def kv_specs(col):
        prev = pl.BlockSpec((BLOCK, D_KV), lambda b, i: (b * nb + jnp.maximum(i * blocks - 1, 0), col))
        cur = pl.BlockSpec((tq, D_KV), lambda b, i: (b * ns + i, col))
        nxt = pl.BlockSpec((BLOCK, D_KV), lambda b, i: (b * nb + jnp.minimum((i + 1) * blocks, nb - 1), col))
        return [prev, cur, nxt]

    return pl.pallas_call(
        functools.partial(_attn_kernel, blocks=blocks),
        grid=(batch, ns),
        in_specs=[pl.BlockSpec((tq, D_ATTN), lambda b, i: (b * ns + i, q_col))] + kv_specs(k_col) + kv_specs(v_col) + [
            pl.BlockSpec((1, HEAD_DIM), lambda b, i: (0, 0)),
            pl.BlockSpec((1, HEAD_DIM), lambda b, i: (0, 0)),
            pl.BlockSpec((1, N_HEADS), lambda b, i: (0, 0)),
        ],
        out_specs=pl.BlockSpec((tq, D_ATTN), lambda b, i: (b * ns + i, 0)),
        out_shape=jax.ShapeDtypeStruct((n, D_ATTN), F32),
        scratch_shapes=[pltpu.VMEM((3, len(ATTN_GROUPS), ATTN_GROUP_ROWS, 3 * BLOCK), F32)],
        compiler_params=_params("arbitrary", "arbitrary"),
        name="attention",
    )(p, p, p, p, p, p, p, qg, kg, sink)


def _outproj_kernel(x_ref, yh_ref, ys_ref, ya_ref, g_ref, w_ref, o_ref):
    d1 = D_HYENA
    d2 = D_HYENA + D_SCONV
    nh = _rms(yh_ref[...], g_ref[:, :d1]).astype(BF16)
    ns = _rms(ys_ref[...], g_ref[:, d1:d2]).astype(BF16)
    na = _rms(ya_ref[...], g_ref[:, d2:]).astype(BF16)
    acc = jnp.dot(nh, w_ref[:d1, :], preferred_element_type=F32)
    acc += jnp.dot(ns, w_ref[d1:d2, :], preferred_element_type=F32)
    acc += jnp.dot(na, w_ref[d2:, :], preferred_element_type=F32)
    o_ref[...] = x_ref[...] + acc


def _outproj(x, yh, ys, ya, g, w, *, layer, tm):
    n, d = x.shape
    row = lambda i: (i, 0)
    return pl.pallas_call(
        _outproj_kernel,
        grid=(n // tm,),
        in_specs=[
            pl.BlockSpec((tm, d), row),
            pl.BlockSpec((tm, D_HYENA), row),
            pl.BlockSpec((tm, D_SCONV), row),
            pl.BlockSpec((tm, D_ATTN), row),
            pl.BlockSpec((1, w.shape[1]), lambda i: (0, 0)),
            pl.BlockSpec((None,) + w.shape[1:], lambda i: (layer, 0, 0)),
        ],
        out_specs=pl.BlockSpec((tm, d), row),
        out_shape=jax.ShapeDtypeStruct((n, d), F32),
        compiler_params=_params("parallel"),
        name="outproj",
    )(x, yh, ys, ya, g, w)


def _tile(n, want):
    t = min(n, want)
    assert n % t == 0
    return t


def _filter_features(seq):
    n = jnp.arange(seq, dtype=F32)
    t = n / float(max(seq - 1, 1))
    bands = jnp.linspace(1e-4, FILTER_BANDS - 1, FILTER_BANDS, dtype=F32)
    ang = (2.0 * math.pi / seq) * n[:, None] * bands[None, :]
    z = jnp.concatenate([t[:, None], jnp.cos(ang), jnp.sin(ang)], axis=-1)
    return jnp.pad(z, ((0, 0), (0, 128 - z.shape[1])))


def kernel(x, norm_ffn1, ffn1_w_gate, ffn1_w_up, ffn1_w_down, norm_mix, w_in, hyena_short_w, hyena_short_b,
           filt_w1, filt_b1, filt_w2, filt_b2, filt_w3, filt_b3, filt_freq, filt_w_out, hyena_bias, sconv_w,
           q_norm_g, k_norm_g, attn_sink, mix_out_g, w_out, norm_ffn2, ffn2_w_gate, ffn2_w_up, ffn2_w_down):
    batch, seq, d = x.shape
    depth = w_in.shape[0]
    n = batch * seq
    tm = _tile(n, 1024)
    tf = _tile(ffn1_w_gate.shape[2], 512)
    tn = _tile(w_in.shape[2], 1536)
    tl = _tile(seq, 512)
    tk = _tile(seq, 512)

    xf = x.reshape(n, d)
    row = lambda a: a.reshape(1, -1)
    c_tab, s_tab = _dft_tables(seq, rows=_tile(seq, 256))
    z = _filter_features(seq)
    deltas = jnp.abs(jnp.linspace(HYENA_MIN_DECAY, HYENA_MAX_DECAY, D_HYENA, dtype=F32)).reshape(1, -1)

    wg1, wu1, wd1, wg2, wu2, wd2, w_in_b, w_out_b = (
        _cast_bf16(w) for w in (ffn1_w_gate, ffn1_w_up, ffn1_w_down, ffn2_w_gate, ffn2_w_up, ffn2_w_down, w_in, w_out))

    for l in range(depth):
        xf = _ffn(xf, row(norm_ffn1[l]), wg1, wu1, wd1, layer=l, tm=tm, tf=tf)

        p = _inproj(xf, row(norm_mix[l]), w_in_b, layer=l, tm=tm, tn=tn)

        w1 = jnp.pad(filt_w1[l], ((0, z.shape[1] - filt_w1.shape[1]), (0, 0)))
        kp, km, ny = _hyena_filter(z, w1, row(filt_b1[l]), filt_w2[l], row(filt_b2[l]), filt_w3[l], row(filt_b3[l]),
                                   row(filt_freq[l]), filt_w_out[l], deltas, rows=tk)
        ak, bk = _filter_dft(c_tab, s_tab, kp, km, rows=tk)

        zv, zb, x0, y_sc = _mixprep(p, hyena_short_w[l], row(hyena_short_b[l]), sconv_w[l], batch=batch, seq=seq, tl=tl)
        yc, ys, yn = _hyena_fwd(c_tab, s_tab, zb, ak, bk, ny, batch=batch, rows=tk)
        y_hy = _hyena_inv(c_tab, s_tab, yc, ys, yn, zv, x0, row(hyena_bias[l]), batch=batch, rows=tk)

        y_at = _attention(p, row(q_norm_g[l]), row(k_norm_g[l]), row(attn_sink[l]), batch=batch, seq=seq,
                          blocks=_tile(seq // BLOCK, 4))

        xf = _outproj(xf, y_hy, y_sc, y_at, row(mix_out_g[l]), w_out_b, layer=l, tm=_tile(n, 512))

        xf = _ffn(xf, row(norm_ffn2[l]), wg2, wu2, wd2, layer=l, tm=tm, tf=tf)

    return xf.reshape(batch, seq, d)
```

```python
import functools
import math

import jax
import jax.numpy as jnp
from jax import lax
from jax.experimental import pallas as pl
from jax.experimental.pallas import tpu as pltpu

F32 = jnp.float32
BF16 = jnp.bfloat16

EPS = 1e-6
NEG_INF = -1e30

D_HYENA = 512
D_SCONV = 512
N_HEADS = 16
N_KV_HEADS = 4
GROUP = N_HEADS // N_KV_HEADS
HEAD_DIM = 64
D_ATTN = N_HEADS * HEAD_DIM
D_KV = N_KV_HEADS * HEAD_DIM
WINDOW = 128
BLOCK = 128
FILTER_BANDS = 16
FILTER_HIDDEN = 64
HYENA_MIN_DECAY = math.log(1e-2) / 1.5
HYENA_MAX_DECAY = math.log(1e-2) / 0.3
HYENA_SHIFT = 0.05

V7X_VMEM_LIMIT_BYTES = 60 * 1024 * 1024
HALO_ROWS = 16
LANES = 128
F32_TILE_ROWS = 8

FFT_B = 32
FFT_D = FFT_B // 2
PERM_ROWS = FFT_B * HALO_ROWS
FFT_LANES = 256


def _params(*sem):
    return pltpu.CompilerParams(dimension_semantics=sem, vmem_limit_bytes=V7X_VMEM_LIMIT_BYTES)


def _rms(x, g):
    ms = jnp.mean(x * x, axis=-1, keepdims=True)
    return x * lax.rsqrt(ms + EPS) * g


CAST_BLOCK_BYTES = 6 * 1024 * 1024


def _cast_kernel(w_ref, o_ref):
    o_ref[...] = w_ref[...].astype(BF16)


def _cast_bf16(w):
    depth, r, c = w.shape
    rows = r
    while rows * c * 4 > CAST_BLOCK_BYTES and rows % 32 == 0:
        rows //= 2
    spec = pl.BlockSpec((None, rows, c), lambda l, i: (l, i, 0))
    return pl.pallas_call(
        _cast_kernel,
        grid=(depth, r // rows),
        in_specs=[spec],
        out_specs=spec,
        out_shape=jax.ShapeDtypeStruct(w.shape, BF16),
        compiler_params=_params("parallel", "parallel"),
        name="cast_bf16",
    )(w)


def _ffn_kernel(x_ref, g_ref, wg_ref, wu_ref, wd_ref, o_ref, h_ref):
    @pl.when(pl.program_id(1) == 0)
    def _():
        x = x_ref[...]
        h_ref[...] = _rms(x, g_ref[...]).astype(BF16)
        o_ref[...] = x

    h = h_ref[...]
    gate = jnp.dot(h, wg_ref[...], preferred_element_type=F32)
    up = jnp.dot(h, wu_ref[...], preferred_element_type=F32)
    a = (gate * jax.nn.sigmoid(gate) * up * 0.5).astype(BF16)
    o_ref[...] += jnp.dot(a, wd_ref[...], preferred_element_type=F32)


def _ffn(x, g, wg, wu, wd, *, layer, tm, tf):
    n, d = x.shape
    ff = wg.shape[2]
    return pl.pallas_call(
        _ffn_kernel,
        grid=(n // tm, ff // tf),
        in_specs=[
            pl.BlockSpec((tm, d), lambda i, j: (i, 0)),
            pl.BlockSpec((1, d), lambda i, j: (0, 0)),
            pl.BlockSpec((None, d, tf), lambda i, j: (layer, 0, j)),
            pl.BlockSpec((None, d, tf), lambda i, j: (layer, 0, j)),
            pl.BlockSpec((None, tf, d), lambda i, j: (layer, j, 0)),
        ],
        out_specs=pl.BlockSpec((tm, d), lambda i, j: (i, 0)),
        out_shape=jax.ShapeDtypeStruct((n, d), F32),
        scratch_shapes=[pltpu.VMEM((tm, d), BF16)],
        compiler_params=_params("parallel", "arbitrary"),
        name="ffn",
    )(x, g, wg, wu, wd)


def _inproj_kernel(x_ref, g_ref, w_ref, o_ref, h_ref):
    @pl.when(pl.program_id(1) == 0)
    def _():
        h_ref[...] = _rms(x_ref[...], g_ref[...]).astype(BF16)

    o_ref[...] = jnp.dot(h_ref[...], w_ref[...], preferred_element_type=F32).astype(o_ref.dtype)


def _inproj(x, g, w, *, layer, tm, tn):
    n, d = x.shape
    dp = w.shape[2]
    return pl.pallas_call(
        _inproj_kernel,
        grid=(n // tm, dp // tn),
        in_specs=[
            pl.BlockSpec((tm, d), lambda i, j: (i, 0)),
            pl.BlockSpec((1, d), lambda i, j: (0, 0)),
            pl.BlockSpec((None, d, tn), lambda i, j: (layer, 0, j)),
        ],
        out_specs=pl.BlockSpec((tm, tn), lambda i, j: (i, j)),
        out_shape=jax.ShapeDtypeStruct((n, dp), BF16),
        scratch_shapes=[pltpu.VMEM((tm, d), BF16)],
        compiler_params=_params("parallel", "arbitrary"),
        name="inproj",
    )(x, g, w)


def _fft_tables(seq):
    period = 2 * seq
    t1 = period // FFT_B
    na = seq // FFT_B
    b = jnp.arange(FFT_B, dtype=jnp.int32)[:, None, None]
    c = jnp.arange(t1, dtype=jnp.int32)[None, :, None]
    a = jnp.arange(na, dtype=jnp.int32)[None, None, :]
    ang = ((c * (FFT_B * a + b)) % period).astype(F32) * (2.0 * math.pi / period)
    f1 = jnp.concatenate([jnp.cos(ang), -jnp.sin(ang)], axis=1)
    d = jnp.arange(FFT_D, dtype=jnp.int32)[:, None]
    bb = jnp.arange(FFT_B, dtype=jnp.int32)[None, :]
    ang2 = ((d * bb) % FFT_B).astype(F32) * (2.0 * math.pi / FFT_B)
    eye = jnp.eye(F32_TILE_ROWS, dtype=F32)
    cr = jnp.kron(jnp.cos(ang2), eye)
    sr = jnp.kron(jnp.sin(ang2), eye)
    mb = jnp.block([[cr, sr], [-sr, cr]])
    mbi = jnp.block([[cr.T, -sr.T], [sr.T, cr.T]])
    r = jnp.arange(PERM_ROWS, dtype=jnp.int32)
    src = (r % HALO_ROWS) * FFT_B + r // HALO_ROWS
    perm = (src[:, None] == r[None, :]).astype(BF16)
    return dict(f1=f1.astype(BF16), f1t=jnp.swapaxes(f1, 1, 2).astype(BF16), mb=mb.astype(BF16),
                mbi=mbi.astype(BF16), perm=perm, unperm=perm.T)


def _filter_kernel(z_ref, w1_ref, b1_ref, w2_ref, b2_ref, w3_ref, b3_ref, fr_ref, wo_ref, dl_ref, perm_ref,
                   kpm_ref, ny_ref, *, seq, rows):
    i = pl.program_id(0)
    hp = lax.Precision.HIGHEST
    fr = fr_ref[...]
    h = jnp.sin(fr * (jnp.dot(z_ref[...], w1_ref[...], precision=hp, preferred_element_type=F32) + b1_ref[...]))
    h = jnp.sin(fr * (jnp.dot(h, w2_ref[...], precision=hp, preferred_element_type=F32) + b2_ref[...]))
    h = jnp.sin(fr * (jnp.dot(h, w3_ref[...], precision=hp, preferred_element_type=F32) + b3_ref[...]))
    h = jnp.dot(h, wo_ref[...], precision=hp, preferred_element_type=F32)
    n = lax.broadcasted_iota(jnp.int32, (rows, 1), 0) + i * rows
    t = n.astype(F32) / float(max(seq - 1, 1))
    window = jnp.exp(-t * dl_ref[...]) + HYENA_SHIFT
    hf = h[:, :D_HYENA] * window
    hb = jnp.where(n == 0, 0.0, h[:, D_HYENA:] * window)
    kp = hf + hb
    perm = perm_ref[...]
    kpm_ref[0] = jnp.dot(perm, kp.astype(BF16), preferred_element_type=F32).astype(BF16)
    kpm_ref[1] = jnp.dot(perm, (hf - hb).astype(BF16), preferred_element_type=F32).astype(BF16)
    sign = jnp.where((n & 1) == 0, 1.0, -1.0)

    @pl.when(i == 0)
    def _():
        ny_ref[...] = jnp.zeros_like(ny_ref)

    ny_ref[...] += jnp.sum(kp * sign, axis=0, keepdims=True)


def _hyena_filter(z, w1, b1, w2, b2, w3, b3, fr, wo, deltas, perm):
    seq, zw = z.shape
    rows = PERM_ROWS
    full = lambda a: pl.BlockSpec(a.shape, lambda i: (0,) * a.ndim)
    return pl.pallas_call(
        functools.partial(_filter_kernel, seq=seq, rows=rows),
        grid=(seq // rows,),
        in_specs=[pl.BlockSpec((rows, zw), lambda i: (i, 0))] + [
            full(a) for a in (w1, b1, w2, b2, w3, b3, fr, wo, deltas, perm)],
        out_specs=[
            pl.BlockSpec((2, rows, D_HYENA), lambda i: (0, i, 0)),
            pl.BlockSpec((1, D_HYENA), lambda i: (0, 0)),
        ],
        out_shape=[
            jax.ShapeDtypeStruct((2, seq, D_HYENA), BF16),
            jax.ShapeDtypeStruct((1, D_HYENA), F32),
        ],
        compiler_params=_params("arbitrary"),
        name="hyena_filter",
    )(z, w1, b1, w2, b2, w3, b3, fr, wo, deltas, perm)


def _shift_rows(u, halo_prev, halo_next, first, last):
    rows = u.shape[0]
    r = lax.broadcasted_iota(jnp.int32, (rows, 1), 0)
    prev_row = jnp.where(first, 0.0, halo_prev)
    next_row = jnp.where(last, 0.0, halo_next)
    um = jnp.where(r == 0, prev_row, pltpu.roll(u, 1, axis=0))
    up = jnp.where(r == rows - 1, next_row, pltpu.roll(u, rows - 1, axis=0))
    return um, up


def _mixprep_kernel(p_ref, pp_ref, pn_ref, hw_ref, hb_ref, sw_ref, perm_ref, zv_ref, zp_ref, x0_ref, ysc_ref):
    first = pl.program_id(1) == 0
    last = pl.program_id(1) == pl.num_programs(1) - 1
    nh = 3 * D_HYENA
    hr = HALO_ROWS
    p = p_ref[:, :nh].astype(F32)
    pm, pp = _shift_rows(p, pp_ref[hr - 1:hr, :nh].astype(F32), pn_ref[0:1, :nh].astype(F32), first, last)
    u = pm * hw_ref[0:1, :] + p * hw_ref[1:2, :] + pp * hw_ref[2:3, :] + hb_ref[...]
    zv = u[:, 2 * D_HYENA:] * u[:, D_HYENA:2 * D_HYENA]
    zv_ref[...] = zv
    zp_ref[...] = jnp.dot(perm_ref[...], zv.astype(BF16), preferred_element_type=F32).astype(BF16)
    x0_ref[...] = u[:, :D_HYENA]
    ds = D_SCONV
    gb = p_ref[:, nh:nh + ds].astype(F32)
    ch = p_ref[:, nh + ds:nh + 2 * ds].astype(F32) * p_ref[:, nh + 2 * ds:nh + 3 * ds].astype(F32)
    ch_prev = pp_ref[hr - 1:hr, nh + ds:nh + 2 * ds].astype(F32) * pp_ref[hr - 1:hr, nh + 2 * ds:nh + 3 * ds].astype(F32)
    ch_next = pn_ref[0:1, nh + ds:nh + 2 * ds].astype(F32) * pn_ref[0:1, nh + 2 * ds:nh + 3 * ds].astype(F32)
    cm, cp = _shift_rows(ch, ch_prev, ch_next, first, last)
    ysc_ref[...] = gb * (cm * sw_ref[0:1, :] + ch * sw_ref[1:2, :] + cp * sw_ref[2:3, :])


def _mixprep(p, hw, hb, sw, perm, *, batch, seq):
    n = p.shape[0]
    tl = PERM_ROWS
    nl = seq // tl
    width = 3 * D_HYENA + 3 * D_SCONV
    hb8 = tl // HALO_ROWS
    last8 = seq // HALO_ROWS - 1
    row = lambda b, l: (b * nl + l, 0)
    const = lambda a: pl.BlockSpec(a.shape, lambda b, l: (0, 0))
    return pl.pallas_call(
        _mixprep_kernel,
        grid=(batch, nl),
        in_specs=[
            pl.BlockSpec((tl, width), row),
            pl.BlockSpec((HALO_ROWS, width), lambda b, l: (b * (last8 + 1) + jnp.maximum(l * hb8 - 1, 0), 0)),
            pl.BlockSpec((HALO_ROWS, width), lambda b, l: (b * (last8 + 1) + jnp.minimum((l + 1) * hb8, last8), 0)),
            const(hw), const(hb), const(sw), const(perm),
        ],
        out_specs=[pl.BlockSpec((tl, D_HYENA), row)] * 4,
        out_shape=[
            jax.ShapeDtypeStruct((n, D_HYENA), F32),
            jax.ShapeDtypeStruct((n, D_HYENA), BF16),
            jax.ShapeDtypeStruct((n, D_HYENA), F32),
            jax.ShapeDtypeStruct((n, D_SCONV), F32),
        ],
        compiler_params=_params("parallel", "parallel"),
        name="mixprep",
    )(p, p, p, hw, hb, sw, perm)


def _fft_forward(zp_ref, f1_ref, mb_ref, a_scr, *, seq, emit):
    t1 = 2 * seq // FFT_B
    r8 = F32_TILE_ROWS
    for b in range(FFT_B):
        xb = jnp.concatenate([zp_ref[blk * PERM_ROWS + b * HALO_ROWS:blk * PERM_ROWS + (b + 1) * HALO_ROWS, :]
                              for blk in range(seq // PERM_ROWS)], axis=0)
        a_scr[b] = jnp.dot(f1_ref[b], xb, preferred_element_type=F32)
    mb = mb_ref[...]
    half = FFT_D * r8
    for ch in range(t1 // r8):
        re = [a_scr[b, ch * r8:(ch + 1) * r8, :] for b in range(FFT_B)]
        im = [a_scr[b, t1 + ch * r8:t1 + (ch + 1) * r8, :] for b in range(FFT_B)]
        xs = jnp.dot(mb, jnp.concatenate(re + im, axis=0).astype(BF16), preferred_element_type=F32)
        emit(ch, xs[:half], xs[half:])


def _filter_fft_kernel(kpm_ref, f1_ref, mb_ref, k_ref, a_scr, *, seq):
    part = pl.program_id(0)
    half = FFT_D * F32_TILE_ROWS

    def emit(ch, re, im):
        val = jnp.where(part == 0, re, im) * (1.0 / seq)
        if ch == 0:
            val = jnp.where(lax.broadcasted_iota(jnp.int32, (half, 1), 0) == 0, 0.5 * val, val)
        k_ref[ch * half:(ch + 1) * half, :] = val.astype(BF16)

    _fft_forward(kpm_ref, f1_ref, mb_ref, a_scr, seq=seq, emit=emit)


def _filter_fft(kpm, tabs):
    _, seq, width = kpm.shape
    w = FFT_LANES
    f1, mb = tabs["f1"], tabs["mb"]
    return pl.pallas_call(
        functools.partial(_filter_fft_kernel, seq=seq),
        grid=(2, width // w),
        in_specs=[
            pl.BlockSpec((None, seq, w), lambda p, h: (p, 0, h)),
            pl.BlockSpec(f1.shape, lambda p, h: (0, 0, 0)),
            pl.BlockSpec(mb.shape, lambda p, h: (0, 0)),
        ],
        out_specs=pl.BlockSpec((None, seq, w), lambda p, h: (p, 0, h)),
        out_shape=jax.ShapeDtypeStruct((2, seq, width), BF16),
        scratch_shapes=[pltpu.VMEM((FFT_B, f1.shape[1], w), F32)],
        compiler_params=_params("parallel", "parallel"),
        name="filter_fft",
    )(kpm, f1, mb)


def _hyena_fwd_kernel(zp_ref, f1_ref, mb_ref, k_ref, ny_ref, y_ref, yn_ref, a_scr, *, seq):
    half = FFT_D * F32_TILE_ROWS
    r = lax.broadcasted_iota(jnp.int32, (seq, 1), 0)
    sign = jnp.where(((r // HALO_ROWS) & 1) == 0, 1.0, -1.0)
    xn = jnp.sum(zp_ref[...].astype(F32) * sign, axis=0, keepdims=True)
    yn_ref[...] = (xn * ny_ref[...] * (0.5 / seq)).reshape(yn_ref.shape)

    def emit(ch, re, im):
        kr = k_ref[0, ch * half:(ch + 1) * half, :].astype(F32)
        ki = k_ref[1, ch * half:(ch + 1) * half, :].astype(F32)
        y_ref[2 * ch * half:(2 * ch + 1) * half, :] = (re * kr - im * ki).astype(BF16)
        y_ref[(2 * ch + 1) * half:(2 * ch + 2) * half, :] = (re * ki + im * kr).astype(BF16)

    _fft_forward(zp_ref, f1_ref, mb_ref, a_scr, seq=seq, emit=emit)


def _hyena_fwd(zp, k, ny, tabs, *, batch, seq):
    width = zp.shape[1]
    w = FFT_LANES
    f1, mb = tabs["f1"], tabs["mb"]
    return pl.pallas_call(
        functools.partial(_hyena_fwd_kernel, seq=seq),
        grid=(batch, width // w),
        in_specs=[
            pl.BlockSpec((seq, w), lambda b, h: (b, h)),
            pl.BlockSpec(f1.shape, lambda b, h: (0, 0, 0)),
            pl.BlockSpec(mb.shape, lambda b, h: (0, 0)),
            pl.BlockSpec((2, seq, w), lambda b, h: (0, 0, h)),
            pl.BlockSpec((1, w), lambda b, h: (0, h)),
        ],
        out_specs=[
            pl.BlockSpec((2 * seq, w), lambda b, h: (b, h)),
            pl.BlockSpec((1, 1, w), lambda b, h: (b, 0, h)),
        ],
        out_shape=[
            jax.ShapeDtypeStruct((batch * 2 * seq, width), BF16),
            jax.ShapeDtypeStruct((batch, 1, width), F32),
        ],
        scratch_shapes=[pltpu.VMEM((FFT_B, f1.shape[1], w), F32)],
        compiler_params=_params("parallel", "parallel"),
        name="hyena_fwd",
    )(zp, f1, mb, k, ny)


def _hyena_inv_kernel(y_ref, mbi_ref, f1t_ref, o_ref, g_scr, *, seq):
    t1 = 2 * seq // FFT_B
    r8 = F32_TILE_ROWS
    nch = t1 // r8
    blk_rows = 2 * FFT_D * r8
    mbi = mbi_ref[...]
    for ch in range(nch):
        g_scr[ch] = jnp.dot(mbi, y_ref[ch * blk_rows:(ch + 1) * blk_rows, :], preferred_element_type=F32)
    half = FFT_B * r8
    for b in range(FFT_B):
        re = [g_scr[ch, b * r8:(b + 1) * r8, :] for ch in range(nch)]
        im = [g_scr[ch, half + b * r8:half + (b + 1) * r8, :] for ch in range(nch)]
        yb = jnp.dot(f1t_ref[b], jnp.concatenate(re + im, axis=0).astype(BF16), preferred_element_type=F32)
        for blk in range(seq // PERM_ROWS):
            o_ref[blk * PERM_ROWS + b * HALO_ROWS:blk * PERM_ROWS + (b + 1) * HALO_ROWS, :] = (
                yb[blk * HALO_ROWS:(blk + 1) * HALO_ROWS, :])


def _hyena_inv(y, tabs, *, batch, seq):
    width = y.shape[1]
    w = FFT_LANES
    f1t, mbi = tabs["f1t"], tabs["mbi"]
    return pl.pallas_call(
        functools.partial(_hyena_inv_kernel, seq=seq),
        grid=(batch, width // w),
        in_specs=[
            pl.BlockSpec((2 * seq, w), lambda b, h: (b, h)),
            pl.BlockSpec(mbi.shape, lambda b, h: (0, 0)),
            pl.BlockSpec(f1t.shape, lambda b, h: (0, 0, 0)),
        ],
        out_specs=pl.BlockSpec((seq, w), lambda b, h: (b, h)),
        out_shape=jax.ShapeDtypeStruct((batch * seq, width), F32),
        scratch_shapes=[pltpu.VMEM((2 * seq // FFT_B // F32_TILE_ROWS, 2 * FFT_B * F32_TILE_ROWS, w), F32)],
        compiler_params=_params("parallel", "parallel"),
        name="hyena_inv",
    )(y, mbi, f1t)


def _hyena_post_kernel(yp_ref, unperm_ref, yn_ref, zv_ref, x0_ref, bd_ref, o_ref):
    yp = yp_ref[...]
    hi = yp.astype(BF16)
    rest = yp - hi.astype(F32)
    mid = rest.astype(BF16)
    lo = (rest - mid.astype(F32)).astype(BF16)
    unperm = unperm_ref[...]
    y = jnp.dot(unperm, hi, preferred_element_type=F32)
    y += jnp.dot(unperm, mid, preferred_element_type=F32)
    y += jnp.dot(unperm, lo, preferred_element_type=F32)
    t = lax.broadcasted_iota(jnp.int32, (PERM_ROWS, 1), 0)
    sign = jnp.where((t & 1) == 0, 1.0, -1.0)
    o_ref[...] = x0_ref[...] * (y + sign * yn_ref[0] + zv_ref[...] * bd_ref[...])


def _hyena_post(yp, unperm, yn, zv, x0, bd, *, batch, seq):
    nl = seq // PERM_ROWS
    row = lambda b, l: (b * nl + l, 0)
    blk = pl.BlockSpec((PERM_ROWS, D_HYENA), row)
    return pl.pallas_call(
        _hyena_post_kernel,
        grid=(batch, nl),
        in_specs=[
            blk,
            pl.BlockSpec(unperm.shape, lambda b, l: (0, 0)),
            pl.BlockSpec((1, 1, D_HYENA), lambda b, l: (b, 0, 0)),
            blk, blk,
            pl.BlockSpec((1, D_HYENA), lambda b, l: (0, 0)),
        ],
        out_specs=blk,
        out_shape=jax.ShapeDtypeStruct((batch * seq, D_HYENA), F32),
        compiler_params=_params("parallel", "parallel"),
        name="hyena_post",
    )(yp, unperm, yn, zv, x0, bd)


HEADS_PER_TILE = LANES // HEAD_DIM
ATTN_GROUPS = tuple(
    tuple(h for h in range(N_HEADS)
          if (h // GROUP) // HEADS_PER_TILE == tile and (h % HEADS_PER_TILE != (h // GROUP) % HEADS_PER_TILE) == swapped)
    for tile in range(N_KV_HEADS // HEADS_PER_TILE) for swapped in (False, True))
ATTN_GROUP_ROWS = len(ATTN_GROUPS[0]) * BLOCK


def _attn_kernel(q_ref, kp_ref, kc_ref, kn_ref, vp_ref, vc_ref, vn_ref, qg_ref, kg_ref, sink_ref, o_ref, bias_ref,
                 *, blocks):
    step = pl.program_id(1)
    last_step = pl.num_programs(1) - 1
    keys = 3 * BLOCK
    grows = ATTN_GROUP_ROWS

    @pl.when((pl.program_id(0) == 0) & (step == 0))
    def _():
        qi = lax.broadcasted_iota(jnp.int32, (BLOCK, keys), 0)
        kj = lax.broadcasted_iota(jnp.int32, (BLOCK, keys), 1)
        dist = jnp.abs(qi - (kj - BLOCK))
        distf = dist.astype(F32)
        for variant, (lo, hi) in enumerate(((BLOCK, keys), (0, keys), (0, 2 * BLOCK))):
            ok = (dist <= WINDOW) & (kj >= lo) & (kj < hi)
            for g, heads in enumerate(ATTN_GROUPS):
                for j, h in enumerate(heads):
                    slope = 2.0 ** (-8.0 * (h + 1) / N_HEADS)
                    bias_ref[variant, g, j * BLOCK:(j + 1) * BLOCK, :] = jnp.where(ok, -slope * distf, NEG_INF)

    seg_r = lax.broadcasted_iota(jnp.int32, (D_KV, D_KV), 0) // HEAD_DIM
    seg_c = lax.broadcasted_iota(jnp.int32, (D_KV, D_KV), 1) // HEAD_DIM
    seg_mean = jnp.where(seg_r == seg_c, 1.0 / HEAD_DIM, 0.0).astype(BF16)

    def head_norm(x, g):
        ms = jnp.dot((x * x).astype(BF16), seg_mean, preferred_element_type=F32)
        return x * lax.rsqrt(ms + EPS) * g

    low = lax.broadcasted_iota(jnp.int32, (1, LANES), 1) < HEAD_DIM
    qg = jnp.concatenate([qg_ref[...]] * GROUP, axis=1) * (HEAD_DIM ** -0.5)
    kg = jnp.concatenate([kg_ref[...]] * N_KV_HEADS, axis=1)
    k = jnp.concatenate([kp_ref[...], kc_ref[...], kn_ref[...]], axis=0).astype(F32)
    k = head_norm(k, kg)
    v = jnp.concatenate([vp_ref[...], vc_ref[...], vn_ref[...]], axis=0).astype(F32)
    k_var, v_var = [], []
    for t in range(D_KV // LANES):
        kt = k[:, t * LANES:(t + 1) * LANES]
        vt = v[:, t * LANES:(t + 1) * LANES]
        k_var += [kt.astype(BF16), pltpu.roll(kt, HEAD_DIM, axis=1).astype(BF16)]
        v_var += [vt.astype(BF16), pltpu.roll(vt, HEAD_DIM, axis=1).astype(BF16)]

    q_half = {}
    for c in range(N_KV_HEADS):
        qn = head_norm(q_ref[:, c * D_KV:(c + 1) * D_KV].astype(F32), qg)
        for u in range(D_KV // LANES):
            qt = qn[:, u * LANES:(u + 1) * LANES]
            tile = c * (D_KV // LANES) + u
            q_half[tile, 0] = jnp.where(low, qt, 0.0).astype(BF16)
            q_half[tile, 1] = jnp.where(low, 0.0, qt).astype(BF16)

    row_head = lax.broadcasted_iota(jnp.int32, (grows, 1), 0) // BLOCK
    sinks = []
    for heads in ATTN_GROUPS:
        col = jnp.zeros((grows, 1), F32)
        for j, h in enumerate(heads):
            col = jnp.where(row_head == j, sink_ref[0:1, h:h + 1], col)
        sinks.append(col)

    for qb in range(blocks):
        rows = slice(qb * BLOCK, (qb + 1) * BLOCK)
        krows = slice(qb * BLOCK, qb * BLOCK + keys)
        if qb == 0:
            variant = jnp.where(step == 0, 0, 1)
        elif qb == blocks - 1:
            variant = jnp.where(step == last_step, 2, 1)
        else:
            variant = 1
        res = {}
        for g, heads in enumerate(ATTN_GROUPS):
            qs = jnp.concatenate([q_half[h // HEADS_PER_TILE, h % HEADS_PER_TILE][rows] for h in heads], axis=0)
            s = lax.dot_general(qs, k_var[g][krows], (((1,), (1,)), ((), ())), preferred_element_type=F32)
            logits = s + bias_ref[variant, g]
            m = jnp.maximum(jnp.max(logits, axis=-1, keepdims=True), sinks[g])
            p = jnp.exp(logits - m)
            denom = jnp.sum(p, axis=-1, keepdims=True) + jnp.exp(sinks[g] - m)
            o = jnp.dot(p.astype(BF16), v_var[g][krows], preferred_element_type=F32) / denom
            for j, h in enumerate(heads):
                res[h] = o[j * BLOCK:(j + 1) * BLOCK]
        for tile in range(N_HEADS // HEADS_PER_TILE):
            o_ref[rows, tile * LANES:(tile + 1) * LANES] = jnp.where(
                low, res[HEADS_PER_TILE * tile], res[HEADS_PER_TILE * tile + 1])


def _attention(p, qg, kg, sink, *, batch, seq, blocks):
    n = p.shape[0]
    nb = seq // BLOCK
    tq = blocks * BLOCK
    ns = seq // tq
    assert HEADS_PER_TILE == 2 and blocks >= 2 and seq % tq == 0
    q_col = (3 * D_HYENA + 3 * D_SCONV) // D_ATTN
    k_col = (3 * D_HYENA + 3 * D_SCONV + D_ATTN) // D_KV
    v_col = k_col + 1
    assert q_col * D_ATTN == 3 * D_HYENA + 3 * D_SCONV and k_col * D_KV == q_col * D_ATTN + D_ATTN

    def kv_specs(col):
        prev = pl.BlockSpec((BLOCK, D_KV), lambda b, i: (b * nb + jnp.maximum(i * blocks - 1, 0), col))
        cur = pl.BlockSpec((tq, D_KV), lambda b, i: (b * ns + i, col))
        nxt = pl.BlockSpec((BLOCK, D_KV), lambda b, i: (b * nb + jnp.minimum((i + 1) * blocks, nb - 1), col))
        return [prev, cur, nxt]

    return pl.pallas_call(
        functools.partial(_attn_kernel, blocks=blocks),
        grid=(batch, ns),
        in_specs=[pl.BlockSpec((tq, D_ATTN), lambda b, i: (b * ns + i, q_col))] + kv_specs(k_col) + kv_specs(v_col) + [
            pl.BlockSpec((1, HEAD_DIM), lambda b, i: (0, 0)),
            pl.BlockSpec((1, HEAD_DIM), lambda b, i: (0, 0)),
            pl.BlockSpec((1, N_HEADS), lambda b, i: (0, 0)),
        ],
        out_specs=pl.BlockSpec((tq, D_ATTN), lambda b, i: (b * ns + i, 0)),
        out_shape=jax.ShapeDtypeStruct((n, D_ATTN), F32),
        scratch_shapes=[pltpu.VMEM((3, len(ATTN_GROUPS), ATTN_GROUP_ROWS, 3 * BLOCK), F32)],
        compiler_params=_params("arbitrary", "arbitrary"),
        name="attention",
    )(p, p, p, p, p, p, p, qg, kg, sink)


def _outproj_kernel(x_ref, yh_ref, ys_ref, ya_ref, g_ref, w_ref, o_ref):
    d1 = D_HYENA
    d2 = D_HYENA + D_SCONV
    nh = _rms(yh_ref[...], g_ref[:, :d1]).astype(BF16)
    ns = _rms(ys_ref[...], g_ref[:, d1:d2]).astype(BF16)
    na = _rms(ya_ref[...], g_ref[:, d2:]).astype(BF16)
    acc = jnp.dot(nh, w_ref[:d1, :], preferred_element_type=F32)
    acc += jnp.dot(ns, w_ref[d1:d2, :], preferred_element_type=F32)
    acc += jnp.dot(na, w_ref[d2:, :], preferred_element_type=F32)
    o_ref[...] = x_ref[...] + acc


def _outproj(x, yh, ys, ya, g, w, *, layer, tm):
    n, d = x.shape
    row = lambda i: (i, 0)
    return pl.pallas_call(
        _outproj_kernel,
        grid=(n // tm,),
        in_specs=[
            pl.BlockSpec((tm, d), row),
            pl.BlockSpec((tm, D_HYENA), row),
            pl.BlockSpec((tm, D_SCONV), row),
            pl.BlockSpec((tm, D_ATTN), row),
            pl.BlockSpec((1, w.shape[1]), lambda i: (0, 0)),
            pl.BlockSpec((None,) + w.shape[1:], lambda i: (layer, 0, 0)),
        ],
        out_specs=pl.BlockSpec((tm, d), row),
        out_shape=jax.ShapeDtypeStruct((n, d), F32),
        compiler_params=_params("parallel"),
        name="outproj",
    )(x, yh, ys, ya, g, w)


def _tile(n, want):
    t = min(n, want)
    assert n % t == 0
    return t


def _filter_features(seq):
    n = jnp.arange(seq, dtype=F32)
    t = n / float(max(seq - 1, 1))
    bands = jnp.linspace(1e-4, FILTER_BANDS - 1, FILTER_BANDS, dtype=F32)
    ang = (2.0 * math.pi / seq) * n[:, None] * bands[None, :]
    z = jnp.concatenate([t[:, None], jnp.cos(ang), jnp.sin(ang)], axis=-1)
    return jnp.pad(z, ((0, 0), (0, LANES - z.shape[1])))


def kernel(x, norm_ffn1, ffn1_w_gate, ffn1_w_up, ffn1_w_down, norm_mix, w_in, hyena_short_w, hyena_short_b,
           filt_w1, filt_b1, filt_w2, filt_b2, filt_w3, filt_b3, filt_freq, filt_w_out, hyena_bias, sconv_w,
           q_norm_g, k_norm_g, attn_sink, mix_out_g, w_out, norm_ffn2, ffn2_w_gate, ffn2_w_up, ffn2_w_down):
    batch, seq, d = x.shape
    depth = w_in.shape[0]
    n = batch * seq
    assert seq % PERM_ROWS == 0
    tm = _tile(n, 1024)
    tf = _tile(ffn1_w_gate.shape[2], 512)
    tn = _tile(w_in.shape[2], 1536)

    xf = x.reshape(n, d)
    row = lambda a: a.reshape(1, -1)
    tabs = _fft_tables(seq)
    z = _filter_features(seq)
    deltas = jnp.abs(jnp.linspace(HYENA_MIN_DECAY, HYENA_MAX_DECAY, D_HYENA, dtype=F32)).reshape(1, -1)

    wg1, wu1, wd1, wg2, wu2, wd2, w_in_b, w_out_b = (
        _cast_bf16(w) for w in (ffn1_w_gate, ffn1_w_up, ffn1_w_down, ffn2_w_gate, ffn2_w_up, ffn2_w_down, w_in, w_out))

    for l in range(depth):
        xf = _ffn(xf, row(norm_ffn1[l]), wg1, wu1, wd1, layer=l, tm=tm, tf=tf)

        p = _inproj(xf, row(norm_mix[l]), w_in_b, layer=l, tm=tm, tn=tn)

        w1 = jnp.pad(filt_w1[l], ((0, z.shape[1] - filt_w1.shape[1]), (0, 0)))
        kpm, ny = _hyena_filter(z, w1, row(filt_b1[l]), filt_w2[l], row(filt_b2[l]), filt_w3[l], row(filt_b3[l]),
                                row(filt_freq[l]), filt_w_out[l], deltas, tabs["perm"])
        kf = _filter_fft(kpm, tabs)

        zv, zp, x0, y_sc = _mixprep(p, hyena_short_w[l], row(hyena_short_b[l]), sconv_w[l], tabs["perm"],
                                    batch=batch, seq=seq)
        y, yn = _hyena_fwd(zp, kf, ny, tabs, batch=batch, seq=seq)
        yp = _hyena_inv(y, tabs, batch=batch, seq=seq)
        y_hy = _hyena_post(yp, tabs["unperm"], yn, zv, x0, row(hyena_bias[l]), batch=batch, seq=seq)

        y_at = _attention(p, row(q_norm_g[l]), row(k_norm_g[l]), row(attn_sink[l]), batch=batch, seq=seq,
                          blocks=_tile(seq // BLOCK, 4))

        xf = _outproj(xf, y_hy, y_sc, y_at, row(mix_out_g[l]), w_out_b, layer=l, tm=_tile(n, 512))

        xf = _ffn(xf, row(norm_ffn2[l]), wg2, wu2, wd2, layer=l, tm=tm, tf=tf)

    return xf.reshape(batch, seq, d)
```

```python
import functools
import math

import jax
import jax.numpy as jnp
from jax import lax
from jax.experimental import pallas as pl
from jax.experimental.pallas import tpu as pltpu

F32 = jnp.float32
BF16 = jnp.bfloat16

EPS = 1e-6
NEG_INF = -1e30

D_HYENA = 512
D_SCONV = 512
N_HEADS = 16
N_KV_HEADS = 4
GROUP = N_HEADS // N_KV_HEADS
HEAD_DIM = 64
D_ATTN = N_HEADS * HEAD_DIM
D_KV = N_KV_HEADS * HEAD_DIM
WINDOW = 128
BLOCK = 128
FILTER_BANDS = 16
FILTER_HIDDEN = 64
HYENA_MIN_DECAY = math.log(1e-2) / 1.5
HYENA_MAX_DECAY = math.log(1e-2) / 0.3
HYENA_SHIFT = 0.05

V7X_VMEM_LIMIT_BYTES = 60 * 1024 * 1024
HALO_ROWS = 16
LANES = 128
F32_TILE_ROWS = 8

FFT_B = 32
FFT_D = FFT_B // 2
PERM_ROWS = FFT_B * HALO_ROWS
FFT_LANES = 256


def _params(*sem):
    return pltpu.CompilerParams(dimension_semantics=sem, vmem_limit_bytes=V7X_VMEM_LIMIT_BYTES)


def _rms(x, g):
    ms = jnp.mean(x * x, axis=-1, keepdims=True)
    return x * lax.rsqrt(ms + EPS) * g


CAST_BLOCK_BYTES = 6 * 1024 * 1024


def _cast_kernel(w_ref, o_ref):
    o_ref[...] = w_ref[...].astype(BF16)


def _cast_bf16(w):
    depth, r, c = w.shape
    rows = r
    while rows * c * 4 > CAST_BLOCK_BYTES and rows % 32 == 0:
        rows //= 2
    spec = pl.BlockSpec((None, rows, c), lambda l, i: (l, i, 0))
    return pl.pallas_call(
        _cast_kernel,
        grid=(depth, r // rows),
        in_specs=[spec],
        out_specs=spec,
        out_shape=jax.ShapeDtypeStruct(w.shape, BF16),
        compiler_params=_params("parallel", "parallel"),
        name="cast_bf16",
    )(w)


def _ffn_kernel(x_ref, g_ref, wg_ref, wu_ref, wd_ref, o_ref, h_ref):
    @pl.when(pl.program_id(1) == 0)
    def _():
        x = x_ref[...]
        h_ref[...] = _rms(x, g_ref[...]).astype(BF16)
        o_ref[...] = x

    h = h_ref[...]
    gate = jnp.dot(h, wg_ref[...], preferred_element_type=F32)
    up = jnp.dot(h, wu_ref[...], preferred_element_type=F32)
    a = (gate * jax.nn.sigmoid(gate) * up * 0.5).astype(BF16)
    o_ref[...] += jnp.dot(a, wd_ref[...], preferred_element_type=F32)


def _ffn(x, g, wg, wu, wd, *, layer, tm, tf):
    n, d = x.shape
    ff = wg.shape[2]
    return pl.pallas_call(
        _ffn_kernel,
        grid=(n // tm, ff // tf),
        in_specs=[
            pl.BlockSpec((tm, d), lambda i, j: (i, 0)),
            pl.BlockSpec((1, d), lambda i, j: (0, 0)),
            pl.BlockSpec((None, d, tf), lambda i, j: (layer, 0, j)),
            pl.BlockSpec((None, d, tf), lambda i, j: (layer, 0, j)),
            pl.BlockSpec((None, tf, d), lambda i, j: (layer, j, 0)),
        ],
        out_specs=pl.BlockSpec((tm, d), lambda i, j: (i, 0)),
        out_shape=jax.ShapeDtypeStruct((n, d), F32),
        scratch_shapes=[pltpu.VMEM((tm, d), BF16)],
        compiler_params=_params("parallel", "arbitrary"),
        name="ffn",
    )(x, g, wg, wu, wd)


def _inproj_kernel(x_ref, g_ref, w_ref, o_ref, h_ref):
    @pl.when(pl.program_id(1) == 0)
    def _():
        h_ref[...] = _rms(x_ref[...], g_ref[...]).astype(BF16)

    o_ref[...] = jnp.dot(h_ref[...], w_ref[...], preferred_element_type=F32).astype(o_ref.dtype)


def _inproj(x, g, w, *, layer, tm, tn):
    n, d = x.shape
    dp = w.shape[2]
    return pl.pallas_call(
        _inproj_kernel,
        grid=(n // tm, dp // tn),
        in_specs=[
            pl.BlockSpec((tm, d), lambda i, j: (i, 0)),
            pl.BlockSpec((1, d), lambda i, j: (0, 0)),
            pl.BlockSpec((None, d, tn), lambda i, j: (layer, 0, j)),
        ],
        out_specs=pl.BlockSpec((tm, tn), lambda i, j: (i, j)),
        out_shape=jax.ShapeDtypeStruct((n, dp), BF16),
        scratch_shapes=[pltpu.VMEM((tm, d), BF16)],
        compiler_params=_params("parallel", "arbitrary"),
        name="inproj",
    )(x, g, w)


def _fft_tables(seq):
    period = 2 * seq
    t1 = period // FFT_B
    na = seq // FFT_B
    b = jnp.arange(FFT_B, dtype=jnp.int32)[:, None, None]
    c = jnp.arange(t1, dtype=jnp.int32)[None, :, None]
    a = jnp.arange(na, dtype=jnp.int32)[None, None, :]
    ang = ((c * (FFT_B * a + b)) % period).astype(F32) * (2.0 * math.pi / period)
    f1 = jnp.concatenate([jnp.cos(ang), -jnp.sin(ang)], axis=1)
    d = jnp.arange(FFT_D, dtype=jnp.int32)[:, None]
    bb = jnp.arange(FFT_B, dtype=jnp.int32)[None, :]
    ang2 = ((d * bb) % FFT_B).astype(F32) * (2.0 * math.pi / FFT_B)
    eye = jnp.eye(F32_TILE_ROWS, dtype=F32)
    cr = jnp.kron(jnp.cos(ang2), eye)
    sr = jnp.kron(jnp.sin(ang2), eye)
    mb = jnp.block([[cr, sr], [-sr, cr]])
    mbi = jnp.block([[cr.T, -sr.T], [sr.T, cr.T]])
    r = jnp.arange(PERM_ROWS, dtype=jnp.int32)
    src = (r % HALO_ROWS) * FFT_B + r // HALO_ROWS
    perm = (src[:, None] == r[None, :]).astype(BF16)
    return dict(f1=f1.astype(BF16), f1t=jnp.swapaxes(f1, 1, 2).astype(BF16), mb=mb.astype(BF16),
                mbi=mbi.astype(BF16), perm=perm, unperm=perm.T)


def _filter_kernel(z_ref, w1_ref, b1_ref, w2_ref, b2_ref, w3_ref, b3_ref, fr_ref, wo_ref, dl_ref, perm_ref,
                   kpm_ref, ny_ref, *, seq, rows):
    i = pl.program_id(0)
    hp = lax.Precision.HIGHEST
    fr = fr_ref[...]
    h = jnp.sin(fr * (jnp.dot(z_ref[...], w1_ref[...], precision=hp, preferred_element_type=F32) + b1_ref[...]))
    h = jnp.sin(fr * (jnp.dot(h, w2_ref[...], precision=hp, preferred_element_type=F32) + b2_ref[...]))
    h = jnp.sin(fr * (jnp.dot(h, w3_ref[...], precision=hp, preferred_element_type=F32) + b3_ref[...]))
    h = jnp.dot(h, wo_ref[...], precision=hp, preferred_element_type=F32)
    n = lax.broadcasted_iota(jnp.int32, (rows, 1), 0) + i * rows
    t = n.astype(F32) / float(max(seq - 1, 1))
    window = jnp.exp(-t * dl_ref[...]) + HYENA_SHIFT
    hf = h[:, :D_HYENA] * window
    hb = jnp.where(n == 0, 0.0, h[:, D_HYENA:] * window)
    kp = hf + hb
    perm = perm_ref[...]
    kpm_ref[0] = jnp.dot(perm, kp.astype(BF16), preferred_element_type=F32).astype(BF16)
    kpm_ref[1] = jnp.dot(perm, (hf - hb).astype(BF16), preferred_element_type=F32).astype(BF16)
    sign = jnp.where((n & 1) == 0, 1.0, -1.0)

    @pl.when(i == 0)
    def _():
        ny_ref[...] = jnp.zeros_like(ny_ref)

    ny_ref[...] += jnp.sum(kp * sign, axis=0, keepdims=True)


def _hyena_filter(z, w1, b1, w2, b2, w3, b3, fr, wo, deltas, perm):
    seq, zw = z.shape
    rows = PERM_ROWS
    full = lambda a: pl.BlockSpec(a.shape, lambda i: (0,) * a.ndim)
    return pl.pallas_call(
        functools.partial(_filter_kernel, seq=seq, rows=rows),
        grid=(seq // rows,),
        in_specs=[pl.BlockSpec((rows, zw), lambda i: (i, 0))] + [
            full(a) for a in (w1, b1, w2, b2, w3, b3, fr, wo, deltas, perm)],
        out_specs=[
            pl.BlockSpec((2, rows, D_HYENA), lambda i: (0, i, 0)),
            pl.BlockSpec((1, D_HYENA), lambda i: (0, 0)),
        ],
        out_shape=[
            jax.ShapeDtypeStruct((2, seq, D_HYENA), BF16),
            jax.ShapeDtypeStruct((1, D_HYENA), F32),
        ],
        compiler_params=_params("arbitrary"),
        name="hyena_filter",
    )(z, w1, b1, w2, b2, w3, b3, fr, wo, deltas, perm)


def _shift_rows(u, halo_prev, halo_next, first, last):
    rows = u.shape[0]
    r = lax.broadcasted_iota(jnp.int32, (rows, 1), 0)
    prev_row = jnp.where(first, 0.0, halo_prev)
    next_row = jnp.where(last, 0.0, halo_next)
    um = jnp.where(r == 0, prev_row, pltpu.roll(u, 1, axis=0))
    up = jnp.where(r == rows - 1, next_row, pltpu.roll(u, rows - 1, axis=0))
    return um, up


def _mixprep_kernel(p_ref, pp_ref, pn_ref, hw_ref, hb_ref, sw_ref, perm_ref, zp_ref, x0p_ref, ysc_ref):
    first = pl.program_id(1) == 0
    last = pl.program_id(1) == pl.num_programs(1) - 1
    nh = 3 * D_HYENA
    hr = HALO_ROWS
    p = p_ref[:, :nh].astype(F32)
    pm, pp = _shift_rows(p, pp_ref[hr - 1:hr, :nh].astype(F32), pn_ref[0:1, :nh].astype(F32), first, last)
    u = pm * hw_ref[0:1, :] + p * hw_ref[1:2, :] + pp * hw_ref[2:3, :] + hb_ref[...]
    zv = u[:, 2 * D_HYENA:] * u[:, D_HYENA:2 * D_HYENA]
    perm = perm_ref[...]
    zp_ref[...] = jnp.dot(perm, zv.astype(BF16), preferred_element_type=F32).astype(BF16)
    x0p_ref[...] = jnp.dot(perm, u[:, :D_HYENA].astype(BF16), preferred_element_type=F32).astype(BF16)
    ds = D_SCONV
    gb = p_ref[:, nh:nh + ds].astype(F32)
    ch = p_ref[:, nh + ds:nh + 2 * ds].astype(F32) * p_ref[:, nh + 2 * ds:nh + 3 * ds].astype(F32)
    ch_prev = pp_ref[hr - 1:hr, nh + ds:nh + 2 * ds].astype(F32) * pp_ref[hr - 1:hr, nh + 2 * ds:nh + 3 * ds].astype(F32)
    ch_next = pn_ref[0:1, nh + ds:nh + 2 * ds].astype(F32) * pn_ref[0:1, nh + 2 * ds:nh + 3 * ds].astype(F32)
    cm, cp = _shift_rows(ch, ch_prev, ch_next, first, last)
    ysc_ref[...] = gb * (cm * sw_ref[0:1, :] + ch * sw_ref[1:2, :] + cp * sw_ref[2:3, :])


def _mixprep(p, hw, hb, sw, perm, *, batch, seq):
    n = p.shape[0]
    tl = PERM_ROWS
    nl = seq // tl
    width = 3 * D_HYENA + 3 * D_SCONV
    hb8 = tl // HALO_ROWS
    last8 = seq // HALO_ROWS - 1
    row = lambda b, l: (b * nl + l, 0)
    const = lambda a: pl.BlockSpec(a.shape, lambda b, l: (0, 0))
    return pl.pallas_call(
        _mixprep_kernel,
        grid=(batch, nl),
        in_specs=[
            pl.BlockSpec((tl, width), row),
            pl.BlockSpec((HALO_ROWS, width), lambda b, l: (b * (last8 + 1) + jnp.maximum(l * hb8 - 1, 0), 0)),
            pl.BlockSpec((HALO_ROWS, width), lambda b, l: (b * (last8 + 1) + jnp.minimum((l + 1) * hb8, last8), 0)),
            const(hw), const(hb), const(sw), const(perm),
        ],
        out_specs=[pl.BlockSpec((tl, D_HYENA), row)] * 3,
        out_shape=[
            jax.ShapeDtypeStruct((n, D_HYENA), BF16),
            jax.ShapeDtypeStruct((n, D_HYENA), BF16),
            jax.ShapeDtypeStruct((n, D_SCONV), F32),
        ],
        compiler_params=_params("parallel", "parallel"),
        name="mixprep",
    )(p, p, p, hw, hb, sw, perm)


def _fft_forward(zp_ref, f1_ref, mb_ref, a_scr, *, seq, emit):
    t1 = 2 * seq // FFT_B
    r8 = F32_TILE_ROWS
    for b in range(FFT_B):
        xb = jnp.concatenate([zp_ref[blk * PERM_ROWS + b * HALO_ROWS:blk * PERM_ROWS + (b + 1) * HALO_ROWS, :]
                              for blk in range(seq // PERM_ROWS)], axis=0)
        a_scr[b] = jnp.dot(f1_ref[b], xb, preferred_element_type=F32)
    mb = mb_ref[...]
    half = FFT_D * r8
    for ch in range(t1 // r8):
        re = [a_scr[b, ch * r8:(ch + 1) * r8, :] for b in range(FFT_B)]
        im = [a_scr[b, t1 + ch * r8:t1 + (ch + 1) * r8, :] for b in range(FFT_B)]
        xs = jnp.dot(mb, jnp.concatenate(re + im, axis=0).astype(BF16), preferred_element_type=F32)
        emit(ch, xs[:half], xs[half:])


def _filter_fft_kernel(kpm_ref, f1_ref, mb_ref, k_ref, a_scr, *, seq):
    part = pl.program_id(0)
    half = FFT_D * F32_TILE_ROWS

    def emit(ch, re, im):
        val = jnp.where(part == 0, re, im) * (1.0 / seq)
        if ch == 0:
            val = jnp.where(lax.broadcasted_iota(jnp.int32, (half, 1), 0) == 0, 0.5 * val, val)
        k_ref[ch * half:(ch + 1) * half, :] = val.astype(BF16)

    _fft_forward(kpm_ref, f1_ref, mb_ref, a_scr, seq=seq, emit=emit)


def _filter_fft(kpm, tabs):
    _, seq, width = kpm.shape
    w = FFT_LANES
    f1, mb = tabs["f1"], tabs["mb"]
    return pl.pallas_call(
        functools.partial(_filter_fft_kernel, seq=seq),
        grid=(2, width // w),
        in_specs=[
            pl.BlockSpec((None, seq, w), lambda p, h: (p, 0, h)),
            pl.BlockSpec(f1.shape, lambda p, h: (0, 0, 0)),
            pl.BlockSpec(mb.shape, lambda p, h: (0, 0)),
        ],
        out_specs=pl.BlockSpec((None, seq, w), lambda p, h: (p, 0, h)),
        out_shape=jax.ShapeDtypeStruct((2, seq, width), BF16),
        scratch_shapes=[pltpu.VMEM((FFT_B, f1.shape[1], w), F32)],
        compiler_params=_params("parallel", "parallel"),
        name="filter_fft",
    )(kpm, f1, mb)


def _hyena_fwd_kernel(zp_ref, f1_ref, mb_ref, k_ref, ny_ref, y_ref, yn_ref, a_scr, *, seq):
    half = FFT_D * F32_TILE_ROWS
    r = lax.broadcasted_iota(jnp.int32, (seq, 1), 0)
    sign = jnp.where(((r // HALO_ROWS) & 1) == 0, 1.0, -1.0)
    xn = jnp.sum(zp_ref[...].astype(F32) * sign, axis=0, keepdims=True)
    yn_ref[...] = (xn * ny_ref[...] * (0.5 / seq)).reshape(yn_ref.shape)

    def emit(ch, re, im):
        kr = k_ref[0, ch * half:(ch + 1) * half, :].astype(F32)
        ki = k_ref[1, ch * half:(ch + 1) * half, :].astype(F32)
        y_ref[2 * ch * half:(2 * ch + 1) * half, :] = (re * kr - im * ki).astype(BF16)
        y_ref[(2 * ch + 1) * half:(2 * ch + 2) * half, :] = (re * ki + im * kr).astype(BF16)

    _fft_forward(zp_ref, f1_ref, mb_ref, a_scr, seq=seq, emit=emit)


def _hyena_fwd(zp, k, ny, tabs, *, batch, seq):
    width = zp.shape[1]
    w = FFT_LANES
    f1, mb = tabs["f1"], tabs["mb"]
    return pl.pallas_call(
        functools.partial(_hyena_fwd_kernel, seq=seq),
        grid=(batch, width // w),
        in_specs=[
            pl.BlockSpec((seq, w), lambda b, h: (b, h)),
            pl.BlockSpec(f1.shape, lambda b, h: (0, 0, 0)),
            pl.BlockSpec(mb.shape, lambda b, h: (0, 0)),
            pl.BlockSpec((2, seq, w), lambda b, h: (0, 0, h)),
            pl.BlockSpec((1, w), lambda b, h: (0, h)),
        ],
        out_specs=[
            pl.BlockSpec((2 * seq, w), lambda b, h: (b, h)),
            pl.BlockSpec((1, 1, w), lambda b, h: (b, 0, h)),
        ],
        out_shape=[
            jax.ShapeDtypeStruct((batch * 2 * seq, width), BF16),
            jax.ShapeDtypeStruct((batch, 1, width), F32),
        ],
        scratch_shapes=[pltpu.VMEM((FFT_B, f1.shape[1], w), F32)],
        compiler_params=_params("parallel", "parallel"),
        name="hyena_fwd",
    )(zp, f1, mb, k, ny)


def _hyena_inv_kernel(y_ref, mbi_ref, f1t_ref, yn_ref, zp_ref, x0p_ref, bd_ref, o_ref, g_scr, *, seq):
    t1 = 2 * seq // FFT_B
    r8 = F32_TILE_ROWS
    nch = t1 // r8
    blk_rows = 2 * FFT_D * r8
    mbi = mbi_ref[...]
    for ch in range(nch):
        g_scr[ch] = jnp.dot(mbi, y_ref[ch * blk_rows:(ch + 1) * blk_rows, :], preferred_element_type=F32)
    half = FFT_B * r8
    yn = yn_ref[0]
    bd = bd_ref[...]
    for b in range(FFT_B):
        re = [g_scr[ch, b * r8:(b + 1) * r8, :] for ch in range(nch)]
        im = [g_scr[ch, half + b * r8:half + (b + 1) * r8, :] for ch in range(nch)]
        yb = jnp.dot(f1t_ref[b], jnp.concatenate(re + im, axis=0).astype(BF16), preferred_element_type=F32)
        nyq = yn if b % 2 == 0 else -yn
        for blk in range(seq // PERM_ROWS):
            rows = slice(blk * PERM_ROWS + b * HALO_ROWS, blk * PERM_ROWS + (b + 1) * HALO_ROWS)
            conv = yb[blk * HALO_ROWS:(blk + 1) * HALO_ROWS, :]
            o_ref[rows, :] = x0p_ref[rows, :].astype(F32) * (conv + nyq + zp_ref[rows, :].astype(F32) * bd)


def _hyena_inv(y, yn, zp, x0p, bd, tabs, *, batch, seq):
    width = y.shape[1]
    w = FFT_LANES
    f1t, mbi = tabs["f1t"], tabs["mbi"]
    blk = pl.BlockSpec((seq, w), lambda b, h: (b, h))
    return pl.pallas_call(
        functools.partial(_hyena_inv_kernel, seq=seq),
        grid=(batch, width // w),
        in_specs=[
            pl.BlockSpec((2 * seq, w), lambda b, h: (b, h)),
            pl.BlockSpec(mbi.shape, lambda b, h: (0, 0)),
            pl.BlockSpec(f1t.shape, lambda b, h: (0, 0, 0)),
            pl.BlockSpec((1, 1, w), lambda b, h: (b, 0, h)),
            blk, blk,
            pl.BlockSpec((1, w), lambda b, h: (0, h)),
        ],
        out_specs=blk,
        out_shape=jax.ShapeDtypeStruct((batch * seq, width), F32),
        scratch_shapes=[pltpu.VMEM((2 * seq // FFT_B // F32_TILE_ROWS, 2 * FFT_B * F32_TILE_ROWS, w), F32)],
        compiler_params=_params("parallel", "parallel"),
        name="hyena_inv",
    )(y, mbi, f1t, yn, zp, x0p, bd)


HEADS_PER_TILE = LANES // HEAD_DIM
ATTN_GROUPS = tuple(
    tuple(h for h in range(N_HEADS)
          if (h // GROUP) // HEADS_PER_TILE == tile and (h % HEADS_PER_TILE != (h // GROUP) % HEADS_PER_TILE) == swapped)
    for tile in range(N_KV_HEADS // HEADS_PER_TILE) for swapped in (False, True))
ATTN_GROUP_ROWS = len(ATTN_GROUPS[0]) * BLOCK


def _attn_kernel(q_ref, kp_ref, kc_ref, kn_ref, vp_ref, vc_ref, vn_ref, qg_ref, kg_ref, sink_ref, o_ref, bias_ref,
                 *, blocks):
    step = pl.program_id(1)
    last_step = pl.num_programs(1) - 1
    keys = 3 * BLOCK
    grows = ATTN_GROUP_ROWS

    @pl.when((pl.program_id(0) == 0) & (step == 0))
    def _():
        qi = lax.broadcasted_iota(jnp.int32, (BLOCK, keys), 0)
        kj = lax.broadcasted_iota(jnp.int32, (BLOCK, keys), 1)
        dist = jnp.abs(qi - (kj - BLOCK))
        distf = dist.astype(F32)
        for variant, (lo, hi) in enumerate(((BLOCK, keys), (0, keys), (0, 2 * BLOCK))):
            ok = (dist <= WINDOW) & (kj >= lo) & (kj < hi)
            for g, heads in enumerate(ATTN_GROUPS):
                for j, h in enumerate(heads):
                    slope = 2.0 ** (-8.0 * (h + 1) / N_HEADS)
                    bias_ref[variant, g, j * BLOCK:(j + 1) * BLOCK, :] = jnp.where(ok, -slope * distf, NEG_INF)

    seg_r = lax.broadcasted_iota(jnp.int32, (D_KV, D_KV), 0) // HEAD_DIM
    seg_c = lax.broadcasted_iota(jnp.int32, (D_KV, D_KV), 1) // HEAD_DIM
    seg_mean = jnp.where(seg_r == seg_c, 1.0 / HEAD_DIM, 0.0).astype(BF16)

    def head_norm(x, g):
        ms = jnp.dot((x * x).astype(BF16), seg_mean, preferred_element_type=F32)
        return x * lax.rsqrt(ms + EPS) * g

    low = lax.broadcasted_iota(jnp.int32, (1, LANES), 1) < HEAD_DIM
    qg = jnp.concatenate([qg_ref[...]] * GROUP, axis=1) * (HEAD_DIM ** -0.5)
    kg = jnp.concatenate([kg_ref[...]] * N_KV_HEADS, axis=1)
    k = jnp.concatenate([kp_ref[...], kc_ref[...], kn_ref[...]], axis=0).astype(F32)
    k = head_norm(k, kg)
    v = jnp.concatenate([vp_ref[...], vc_ref[...], vn_ref[...]], axis=0).astype(F32)
    k_var, v_var = [], []
    for t in range(D_KV // LANES):
        kt = k[:, t * LANES:(t + 1) * LANES]
        vt = v[:, t * LANES:(t + 1) * LANES]
        k_var += [kt.astype(BF16), pltpu.roll(kt, HEAD_DIM, axis=1).astype(BF16)]
        v_var += [vt.astype(BF16), pltpu.roll(vt, HEAD_DIM, axis=1).astype(BF16)]

    q_half = {}
    for c in range(N_KV_HEADS):
        qn = head_norm(q_ref[:, c * D_KV:(c + 1) * D_KV].astype(F32), qg)
        for u in range(D_KV // LANES):
            qt = qn[:, u * LANES:(u + 1) * LANES]
            tile = c * (D_KV // LANES) + u
            q_half[tile, 0] = jnp.where(low, qt, 0.0).astype(BF16)
            q_half[tile, 1] = jnp.where(low, 0.0, qt).astype(BF16)

    row_head = lax.broadcasted_iota(jnp.int32, (grows, 1), 0) // BLOCK
    sinks = []
    for heads in ATTN_GROUPS:
        col = jnp.zeros((grows, 1), F32)
        for j, h in enumerate(heads):
            col = jnp.where(row_head == j, sink_ref[0:1, h:h + 1], col)
        sinks.append(col)

    for qb in range(blocks):
        rows = slice(qb * BLOCK, (qb + 1) * BLOCK)
        krows = slice(qb * BLOCK, qb * BLOCK + keys)
        if qb == 0:
            variant = jnp.where(step == 0, 0, 1)
        elif qb == blocks - 1:
            variant = jnp.where(step == last_step, 2, 1)
        else:
            variant = 1
        res = {}
        for g, heads in enumerate(ATTN_GROUPS):
            qs = jnp.concatenate([q_half[h // HEADS_PER_TILE, h % HEADS_PER_TILE][rows] for h in heads], axis=0)
            s = lax.dot_general(qs, k_var[g][krows], (((1,), (1,)), ((), ())), preferred_element_type=F32)
            logits = s + bias_ref[variant, g]
            m = jnp.maximum(jnp.max(logits, axis=-1, keepdims=True), sinks[g])
            p = jnp.exp(logits - m)
            denom = jnp.sum(p, axis=-1, keepdims=True) + jnp.exp(sinks[g] - m)
            o = jnp.dot(p.astype(BF16), v_var[g][krows], preferred_element_type=F32) / denom
            for j, h in enumerate(heads):
                res[h] = o[j * BLOCK:(j + 1) * BLOCK]
        for tile in range(N_HEADS // HEADS_PER_TILE):
            o_ref[rows, tile * LANES:(tile + 1) * LANES] = jnp.where(
                low, res[HEADS_PER_TILE * tile], res[HEADS_PER_TILE * tile + 1])


def _attention(p, qg, kg, sink, *, batch, seq, blocks):
    n = p.shape[0]
    nb = seq // BLOCK
    tq = blocks * BLOCK
    ns = seq // tq
    assert HEADS_PER_TILE == 2 and blocks >= 2 and seq % tq == 0
    q_col = (3 * D_HYENA + 3 * D_SCONV) // D_ATTN
    k_col = (3 * D_HYENA + 3 * D_SCONV + D_ATTN) // D_KV
    v_col = k_col + 1
    assert q_col * D_ATTN == 3 * D_HYENA + 3 * D_SCONV and k_col * D_KV == q_col * D_ATTN + D_ATTN

    def kv_specs(col):
        prev = pl.BlockSpec((BLOCK, D_KV), lambda b, i: (b * nb + jnp.maximum(i * blocks - 1, 0), col))
        cur = pl.BlockSpec((tq, D_KV), lambda b, i: (b * ns + i, col))
        nxt = pl.BlockSpec((BLOCK, D_KV), lambda b, i: (b * nb + jnp.minimum((i + 1) * blocks, nb - 1), col))
        return [prev, cur, nxt]

    return pl.pallas_call(
        functools.partial(_attn_kernel, blocks=blocks),
        grid=(batch, ns),
        in_specs=[pl.BlockSpec((tq, D_ATTN), lambda b, i: (b * ns + i, q_col))] + kv_specs(k_col) + kv_specs(v_col) + [
            pl.BlockSpec((1, HEAD_DIM), lambda b, i: (0, 0)),
            pl.BlockSpec((1, HEAD_DIM), lambda b, i: (0, 0)),
            pl.BlockSpec((1, N_HEADS), lambda b, i: (0, 0)),
        ],
        out_specs=pl.BlockSpec((tq, D_ATTN), lambda b, i: (b * ns + i, 0)),
        out_shape=jax.ShapeDtypeStruct((n, D_ATTN), F32),
        scratch_shapes=[pltpu.VMEM((3, len(ATTN_GROUPS), ATTN_GROUP_ROWS, 3 * BLOCK), F32)],
        compiler_params=_params("arbitrary", "arbitrary"),
        name="attention",
    )(p, p, p, p, p, p, p, qg, kg, sink)


def _outproj_kernel(x_ref, yh_ref, ys_ref, ya_ref, g_ref, w_ref, unperm_ref, o_ref):
    d1 = D_HYENA
    d2 = D_HYENA + D_SCONV
    nh = _rms(yh_ref[...], g_ref[:, :d1]).astype(BF16)
    nh = jnp.dot(unperm_ref[...], nh, preferred_element_type=F32).astype(BF16)
    ns = _rms(ys_ref[...], g_ref[:, d1:d2]).astype(BF16)
    na = _rms(ya_ref[...], g_ref[:, d2:]).astype(BF16)
    acc = jnp.dot(nh, w_ref[:d1, :], preferred_element_type=F32)
    acc += jnp.dot(ns, w_ref[d1:d2, :], preferred_element_type=F32)
    acc += jnp.dot(na, w_ref[d2:, :], preferred_element_type=F32)
    o_ref[...] = x_ref[...] + acc


def _outproj(x, yh, ys, ya, g, w, unperm, *, layer):
    n, d = x.shape
    tm = PERM_ROWS
    row = lambda i: (i, 0)
    return pl.pallas_call(
        _outproj_kernel,
        grid=(n // tm,),
        in_specs=[
            pl.BlockSpec((tm, d), row),
            pl.BlockSpec((tm, D_HYENA), row),
            pl.BlockSpec((tm, D_SCONV), row),
            pl.BlockSpec((tm, D_ATTN), row),
            pl.BlockSpec((1, w.shape[1]), lambda i: (0, 0)),
            pl.BlockSpec((None,) + w.shape[1:], lambda i: (layer, 0, 0)),
            pl.BlockSpec(unperm.shape, lambda i: (0, 0)),
        ],
        out_specs=pl.BlockSpec((tm, d), row),
        out_shape=jax.ShapeDtypeStruct((n, d), F32),
        compiler_params=_params("parallel"),
        name="outproj",
    )(x, yh, ys, ya, g, w, unperm)


def _tile(n, want):
    t = min(n, want)
    assert n % t == 0
    return t


def _filter_features(seq):
    n = jnp.arange(seq, dtype=F32)
    t = n / float(max(seq - 1, 1))
    bands = jnp.linspace(1e-4, FILTER_BANDS - 1, FILTER_BANDS, dtype=F32)
    ang = (2.0 * math.pi / seq) * n[:, None] * bands[None, :]
    z = jnp.concatenate([t[:, None], jnp.cos(ang), jnp.sin(ang)], axis=-1)
    return jnp.pad(z, ((0, 0), (0, LANES - z.shape[1])))


def kernel(x, norm_ffn1, ffn1_w_gate, ffn1_w_up, ffn1_w_down, norm_mix, w_in, hyena_short_w, hyena_short_b,
           filt_w1, filt_b1, filt_w2, filt_b2, filt_w3, filt_b3, filt_freq, filt_w_out, hyena_bias, sconv_w,
           q_norm_g, k_norm_g, attn_sink, mix_out_g, w_out, norm_ffn2, ffn2_w_gate, ffn2_w_up, ffn2_w_down):
    batch, seq, d = x.shape
    depth = w_in.shape[0]
    n = batch * seq
    assert seq % PERM_ROWS == 0
    tm = _tile(n, 1024)
    tf = _tile(ffn1_w_gate.shape[2], 512)
    tn = _tile(w_in.shape[2], 1536)

    xf = x.reshape(n, d)
    row = lambda a: a.reshape(1, -1)
    tabs = _fft_tables(seq)
    z = _filter_features(seq)
    deltas = jnp.abs(jnp.linspace(HYENA_MIN_DECAY, HYENA_MAX_DECAY, D_HYENA, dtype=F32)).reshape(1, -1)

    wg1, wu1, wd1, wg2, wu2, wd2, w_in_b, w_out_b = (
        _cast_bf16(w) for w in (ffn1_w_gate, ffn1_w_up, ffn1_w_down, ffn2_w_gate, ffn2_w_up, ffn2_w_down, w_in, w_out))

    for l in range(depth):
        xf = _ffn(xf, row(norm_ffn1[l]), wg1, wu1, wd1, layer=l, tm=tm, tf=tf)

        p = _inproj(xf, row(norm_mix[l]), w_in_b, layer=l, tm=tm, tn=tn)

        w1 = jnp.pad(filt_w1[l], ((0, z.shape[1] - filt_w1.shape[1]), (0, 0)))
        kpm, ny = _hyena_filter(z, w1, row(filt_b1[l]), filt_w2[l], row(filt_b2[l]), filt_w3[l], row(filt_b3[l]),
                                row(filt_freq[l]), filt_w_out[l], deltas, tabs["perm"])
        kf = _filter_fft(kpm, tabs)

        zp, x0p, y_sc = _mixprep(p, hyena_short_w[l], row(hyena_short_b[l]), sconv_w[l], tabs["perm"],
                                 batch=batch, seq=seq)
        y, yn = _hyena_fwd(zp, kf, ny, tabs, batch=batch, seq=seq)
        y_hy = _hyena_inv(y, yn, zp, x0p, row(hyena_bias[l]), tabs, batch=batch, seq=seq)

        y_at = _attention(p, row(q_norm_g[l]), row(k_norm_g[l]), row(attn_sink[l]), batch=batch, seq=seq,
                          blocks=_tile(seq // BLOCK, 4))

        xf = _outproj(xf, y_hy, y_sc, y_at, row(mix_out_g[l]), w_out_b, tabs["unperm"], layer=l)

        xf = _ffn(xf, row(norm_ffn2[l]), wg2, wu2, wd2, layer=l, tm=tm, tf=tf)

    return xf.reshape(batch, seq, d)
```

```python
import functools
import math

import jax
import jax.numpy as jnp
from jax import lax
from jax.experimental import pallas as pl
from jax.experimental.pallas import tpu as pltpu

F32 = jnp.float32
BF16 = jnp.bfloat16

EPS = 1e-6
NEG_INF = -1e30

D_HYENA = 512
D_SCONV = 512
N_HEADS = 16
N_KV_HEADS = 4
GROUP = N_HEADS // N_KV_HEADS
HEAD_DIM = 64
D_ATTN = N_HEADS * HEAD_DIM
D_KV = N_KV_HEADS * HEAD_DIM
WINDOW = 128
BLOCK = 128
FILTER_BANDS = 16
FILTER_HIDDEN = 64
HYENA_MIN_DECAY = math.log(1e-2) / 1.5
HYENA_MAX_DECAY = math.log(1e-2) / 0.3
HYENA_SHIFT = 0.05

V7X_VMEM_LIMIT_BYTES = 60 * 1024 * 1024
HALO_ROWS = 16
LANES = 128
F32_TILE_ROWS = 8

FFT_B = 32
FFT_D = FFT_B // 2
PERM_ROWS = FFT_B * HALO_ROWS
FFT_LANES = 256


def _params(*sem):
    return pltpu.CompilerParams(dimension_semantics=sem, vmem_limit_bytes=V7X_VMEM_LIMIT_BYTES)


def _rms(x, g):
    ms = jnp.mean(x * x, axis=-1, keepdims=True)
    return x * lax.rsqrt(ms + EPS) * g


CAST_BLOCK_BYTES = 6 * 1024 * 1024


def _cast_kernel(w_ref, o_ref):
    o_ref[...] = w_ref[...].astype(BF16)


def _cast_bf16(w, layer=None):
    depth, r, c = w.shape
    rows = r
    while rows * c * 4 > CAST_BLOCK_BYTES and rows % 32 == 0:
        rows //= 2
    if layer is None:
        grid = (depth, r // rows)
        in_spec = out_spec = pl.BlockSpec((None, rows, c), lambda l, i: (l, i, 0))
        out_shape = w.shape
    else:
        grid = (r // rows,)
        in_spec = pl.BlockSpec((None, rows, c), lambda i: (layer, i, 0))
        out_spec = pl.BlockSpec((rows, c), lambda i: (i, 0))
        out_shape = (r, c)
    return pl.pallas_call(
        _cast_kernel,
        grid=grid,
        in_specs=[in_spec],
        out_specs=out_spec,
        out_shape=jax.ShapeDtypeStruct(out_shape, BF16),
        compiler_params=_params(*(("parallel",) * len(grid))),
        name="cast_bf16",
    )(w)


def _ffn_kernel(x_ref, g_ref, wg_ref, wu_ref, wd_ref, *rest, n_side):
    side_in, o_ref, side_out, h_ref = rest[:n_side], rest[n_side], rest[n_side + 1:2 * n_side + 1], rest[-1]

    @pl.when(pl.program_id(1) == 0)
    def _():
        x = x_ref[...]
        h_ref[...] = _rms(x, g_ref[...]).astype(BF16)
        o_ref[...] = x

    h = h_ref[...]
    gate = jnp.dot(h, wg_ref[...], preferred_element_type=F32)
    up = jnp.dot(h, wu_ref[...], preferred_element_type=F32)
    a = (gate * jax.nn.sigmoid(gate) * up * 0.5).astype(BF16)
    o_ref[...] += jnp.dot(a, wd_ref[...], preferred_element_type=F32)
    for src, dst in zip(side_in, side_out):
        dst[...] = src[...].astype(BF16)


def _ffn(x, g, wg, wu, wd, *, tm, tf, cast_next=None):
    n, d = x.shape
    ff = wg.shape[1]
    ni, nj = n // tm, ff // tf
    side_args, side_in_specs, side_out_specs, side_shapes = [], [], [], []
    if cast_next is not None:
        *stacked, layer = cast_next
        for w in stacked:
            _, r, c = w.shape
            if r == d:
                blk, idx = (r // ni, c // nj), (lambda i, j: (i, j))
            else:
                blk, idx = (r // nj, c // ni), (lambda i, j: (j, i))
            assert blk[0] % HALO_ROWS == 0 and blk[1] % LANES == 0
            side_args.append(w)
            side_in_specs.append(pl.BlockSpec((None,) + blk, lambda i, j, idx=idx: (layer,) + idx(i, j)))
            side_out_specs.append(pl.BlockSpec(blk, idx))
            side_shapes.append(jax.ShapeDtypeStruct((r, c), BF16))
    out = pl.pallas_call(
        functools.partial(_ffn_kernel, n_side=len(side_args)),
        grid=(ni, nj),
        in_specs=[
            pl.BlockSpec((tm, d), lambda i, j: (i, 0)),
            pl.BlockSpec((1, d), lambda i, j: (0, 0)),
            pl.BlockSpec((d, tf), lambda i, j: (0, j)),
            pl.BlockSpec((d, tf), lambda i, j: (0, j)),
            pl.BlockSpec((tf, d), lambda i, j: (j, 0)),
        ] + side_in_specs,
        out_specs=[pl.BlockSpec((tm, d), lambda i, j: (i, 0))] + side_out_specs,
        out_shape=[jax.ShapeDtypeStruct((n, d), F32)] + side_shapes,
        scratch_shapes=[pltpu.VMEM((tm, d), BF16)],
        compiler_params=_params("parallel", "arbitrary"),
        name="ffn",
    )(x, g, wg, wu, wd, *side_args)
    return out[0], tuple(out[1:])


def _inproj_kernel(x_ref, g_ref, w_ref, o_ref, h_ref):
    @pl.when(pl.program_id(1) == 0)
    def _():
        h_ref[...] = _rms(x_ref[...], g_ref[...]).astype(BF16)

    o_ref[...] = jnp.dot(h_ref[...], w_ref[...], preferred_element_type=F32).astype(o_ref.dtype)


def _inproj(x, g, w, *, layer, tm, tn):
    n, d = x.shape
    dp = w.shape[2]
    return pl.pallas_call(
        _inproj_kernel,
        grid=(n // tm, dp // tn),
        in_specs=[
            pl.BlockSpec((tm, d), lambda i, j: (i, 0)),
            pl.BlockSpec((1, d), lambda i, j: (0, 0)),
            pl.BlockSpec((None, d, tn), lambda i, j: (layer, 0, j)),
        ],
        out_specs=pl.BlockSpec((tm, tn), lambda i, j: (i, j)),
        out_shape=jax.ShapeDtypeStruct((n, dp), BF16),
        scratch_shapes=[pltpu.VMEM((tm, d), BF16)],
        compiler_params=_params("parallel", "arbitrary"),
        name="inproj",
    )(x, g, w)


def _fft_tables(seq):
    period = 2 * seq
    t1 = period // FFT_B
    na = seq // FFT_B
    b = jnp.arange(FFT_B, dtype=jnp.int32)[:, None, None]
    c = jnp.arange(t1, dtype=jnp.int32)[None, :, None]
    a = jnp.arange(na, dtype=jnp.int32)[None, None, :]
    ang = ((c * (FFT_B * a + b)) % period).astype(F32) * (2.0 * math.pi / period)
    f1 = jnp.concatenate([jnp.cos(ang), -jnp.sin(ang)], axis=1)
    d = jnp.arange(FFT_D, dtype=jnp.int32)[:, None]
    bb = jnp.arange(FFT_B, dtype=jnp.int32)[None, :]
    ang2 = ((d * bb) % FFT_B).astype(F32) * (2.0 * math.pi / FFT_B)
    eye = jnp.eye(F32_TILE_ROWS, dtype=F32)
    cr = jnp.kron(jnp.cos(ang2), eye)
    sr = jnp.kron(jnp.sin(ang2), eye)
    mb = jnp.block([[cr, sr], [-sr, cr]])
    mbi = jnp.block([[cr.T, -sr.T], [sr.T, cr.T]])
    r = jnp.arange(PERM_ROWS, dtype=jnp.int32)
    src = (r % HALO_ROWS) * FFT_B + r // HALO_ROWS
    perm = (src[:, None] == r[None, :]).astype(BF16)
    return dict(f1=f1.astype(BF16), f1t=jnp.swapaxes(f1, 1, 2).astype(BF16), mb=mb.astype(BF16),
                mbi=mbi.astype(BF16), perm=perm, unperm=perm.T)


def _filter_kernel(z_ref, w1_ref, b1_ref, w2_ref, b2_ref, w3_ref, b3_ref, fr_ref, wo_ref, dl_ref, perm_ref,
                   kpm_ref, ny_ref, *, seq, rows):
    i = pl.program_id(0)
    hp = lax.Precision.HIGHEST
    fr = fr_ref[...]
    h = jnp.sin(fr * (jnp.dot(z_ref[...], w1_ref[...], precision=hp, preferred_element_type=F32) + b1_ref[...]))
    h = jnp.sin(fr * (jnp.dot(h, w2_ref[...], precision=hp, preferred_element_type=F32) + b2_ref[...]))
    h = jnp.sin(fr * (jnp.dot(h, w3_ref[...], precision=hp, preferred_element_type=F32) + b3_ref[...]))
    h = jnp.dot(h, wo_ref[...], precision=hp, preferred_element_type=F32)
    n = lax.broadcasted_iota(jnp.int32, (rows, 1), 0) + i * rows
    t = n.astype(F32) / float(max(seq - 1, 1))
    window = jnp.exp(-t * dl_ref[...]) + HYENA_SHIFT
    hf = h[:, :D_HYENA] * window
    hb = jnp.where(n == 0, 0.0, h[:, D_HYENA:] * window)
    kp = hf + hb
    perm = perm_ref[...]
    kpm_ref[0] = jnp.dot(perm, kp.astype(BF16), preferred_element_type=F32).astype(BF16)
    kpm_ref[1] = jnp.dot(perm, (hf - hb).astype(BF16), preferred_element_type=F32).astype(BF16)
    sign = jnp.where((n & 1) == 0, 1.0, -1.0)

    @pl.when(i == 0)
    def _():
        ny_ref[...] = jnp.zeros_like(ny_ref)

    ny_ref[...] += jnp.sum(kp * sign, axis=0, keepdims=True)


def _hyena_filter(z, w1, b1, w2, b2, w3, b3, fr, wo, deltas, perm):
    seq, zw = z.shape
    rows = PERM_ROWS
    full = lambda a: pl.BlockSpec(a.shape, lambda i: (0,) * a.ndim)
    return pl.pallas_call(
        functools.partial(_filter_kernel, seq=seq, rows=rows),
        grid=(seq // rows,),
        in_specs=[pl.BlockSpec((rows, zw), lambda i: (i, 0))] + [
            full(a) for a in (w1, b1, w2, b2, w3, b3, fr, wo, deltas, perm)],
        out_specs=[
            pl.BlockSpec((2, rows, D_HYENA), lambda i: (0, i, 0)),
            pl.BlockSpec((1, D_HYENA), lambda i: (0, 0)),
        ],
        out_shape=[
            jax.ShapeDtypeStruct((2, seq, D_HYENA), BF16),
            jax.ShapeDtypeStruct((1, D_HYENA), F32),
        ],
        compiler_params=_params("arbitrary"),
        name="hyena_filter",
    )(z, w1, b1, w2, b2, w3, b3, fr, wo, deltas, perm)


def _shift_rows(u, halo_prev, halo_next, first, last):
    rows = u.shape[0]
    r = lax.broadcasted_iota(jnp.int32, (rows, 1), 0)
    prev_row = jnp.where(first, 0.0, halo_prev)
    next_row = jnp.where(last, 0.0, halo_next)
    um = jnp.where(r == 0, prev_row, pltpu.roll(u, 1, axis=0))
    up = jnp.where(r == rows - 1, next_row, pltpu.roll(u, rows - 1, axis=0))
    return um, up


def _mixprep_kernel(p_ref, pp_ref, pn_ref, hw_ref, hb_ref, sw_ref, perm_ref, zp_ref, x0p_ref, ysc_ref):
    first = pl.program_id(1) == 0
    last = pl.program_id(1) == pl.num_programs(1) - 1
    nh = 3 * D_HYENA
    hr = HALO_ROWS
    p = p_ref[:, :nh].astype(F32)
    pm, pp = _shift_rows(p, pp_ref[hr - 1:hr, :nh].astype(F32), pn_ref[0:1, :nh].astype(F32), first, last)
    u = pm * hw_ref[0:1, :] + p * hw_ref[1:2, :] + pp * hw_ref[2:3, :] + hb_ref[...]
    zv = u[:, 2 * D_HYENA:] * u[:, D_HYENA:2 * D_HYENA]
    perm = perm_ref[...]
    zp_ref[...] = jnp.dot(perm, zv.astype(BF16), preferred_element_type=F32).astype(BF16)
    x0p_ref[...] = jnp.dot(perm, u[:, :D_HYENA].astype(BF16), preferred_element_type=F32).astype(BF16)
    ds = D_SCONV
    gb = p_ref[:, nh:nh + ds].astype(F32)
    ch = p_ref[:, nh + ds:nh + 2 * ds].astype(F32) * p_ref[:, nh + 2 * ds:nh + 3 * ds].astype(F32)
    ch_prev = pp_ref[hr - 1:hr, nh + ds:nh + 2 * ds].astype(F32) * pp_ref[hr - 1:hr, nh + 2 * ds:nh + 3 * ds].astype(F32)
    ch_next = pn_ref[0:1, nh + ds:nh + 2 * ds].astype(F32) * pn_ref[0:1, nh + 2 * ds:nh + 3 * ds].astype(F32)
    cm, cp = _shift_rows(ch, ch_prev, ch_next, first, last)
    ysc_ref[...] = gb * (cm * sw_ref[0:1, :] + ch * sw_ref[1:2, :] + cp * sw_ref[2:3, :])


def _mixprep(p, hw, hb, sw, perm, *, batch, seq):
    n = p.shape[0]
    tl = PERM_ROWS
    nl = seq // tl
    width = 3 * D_HYENA + 3 * D_SCONV
    hb8 = tl // HALO_ROWS
    last8 = seq // HALO_ROWS - 1
    row = lambda b, l: (b * nl + l, 0)
    const = lambda a: pl.BlockSpec(a.shape, lambda b, l: (0, 0))
    return pl.pallas_call(
        _mixprep_kernel,
        grid=(batch, nl),
        in_specs=[
            pl.BlockSpec((tl, width), row),
            pl.BlockSpec((HALO_ROWS, width), lambda b, l: (b * (last8 + 1) + jnp.maximum(l * hb8 - 1, 0), 0)),
            pl.BlockSpec((HALO_ROWS, width), lambda b, l: (b * (last8 + 1) + jnp.minimum((l + 1) * hb8, last8), 0)),
            const(hw), const(hb), const(sw), const(perm),
        ],
        out_specs=[pl.BlockSpec((tl, D_HYENA), row)] * 3,
        out_shape=[
            jax.ShapeDtypeStruct((n, D_HYENA), BF16),
            jax.ShapeDtypeStruct((n, D_HYENA), BF16),
            jax.ShapeDtypeStruct((n, D_SCONV), F32),
        ],
        compiler_params=_params("parallel", "parallel"),
        name="mixprep",
    )(p, p, p, hw, hb, sw, perm)


def _fft_forward(zp_ref, f1_ref, mb_ref, a_scr, *, seq, emit):
    t1 = 2 * seq // FFT_B
    r8 = F32_TILE_ROWS
    for b in range(FFT_B):
        xb = jnp.concatenate([zp_ref[blk * PERM_ROWS + b * HALO_ROWS:blk * PERM_ROWS + (b + 1) * HALO_ROWS, :]
                              for blk in range(seq // PERM_ROWS)], axis=0)
        a_scr[b] = jnp.dot(f1_ref[b], xb, preferred_element_type=F32)
    mb = mb_ref[...]
    half = FFT_D * r8
    for ch in range(t1 // r8):
        re = [a_scr[b, ch * r8:(ch + 1) * r8, :] for b in range(FFT_B)]
        im = [a_scr[b, t1 + ch * r8:t1 + (ch + 1) * r8, :] for b in range(FFT_B)]
        xs = jnp.dot(mb, jnp.concatenate(re + im, axis=0).astype(BF16), preferred_element_type=F32)
        emit(ch, xs[:half], xs[half:])


def _filter_fft_kernel(kpm_ref, f1_ref, mb_ref, k_ref, a_scr, *, seq):
    part = pl.program_id(0)
    half = FFT_D * F32_TILE_ROWS

    def emit(ch, re, im):
        val = jnp.where(part == 0, re, im) * (1.0 / seq)
        if ch == 0:
            val = jnp.where(lax.broadcasted_iota(jnp.int32, (half, 1), 0) == 0, 0.5 * val, val)
        k_ref[ch * half:(ch + 1) * half, :] = val.astype(BF16)

    _fft_forward(kpm_ref, f1_ref, mb_ref, a_scr, seq=seq, emit=emit)


def _filter_fft(kpm, tabs):
    _, seq, width = kpm.shape
    w = FFT_LANES
    f1, mb = tabs["f1"], tabs["mb"]
    return pl.pallas_call(
        functools.partial(_filter_fft_kernel, seq=seq),
        grid=(2, width // w),
        in_specs=[
            pl.BlockSpec((None, seq, w), lambda p, h: (p, 0, h)),
            pl.BlockSpec(f1.shape, lambda p, h: (0, 0, 0)),
            pl.BlockSpec(mb.shape, lambda p, h: (0, 0)),
        ],
        out_specs=pl.BlockSpec((None, seq, w), lambda p, h: (p, 0, h)),
        out_shape=jax.ShapeDtypeStruct((2, seq, width), BF16),
        scratch_shapes=[pltpu.VMEM((FFT_B, f1.shape[1], w), F32)],
        compiler_params=_params("parallel", "parallel"),
        name="filter_fft",
    )(kpm, f1, mb)


def _hyena_fwd_kernel(zp_ref, f1_ref, mb_ref, k_ref, ny_ref, y_ref, yn_ref, a_scr, *, seq):
    half = FFT_D * F32_TILE_ROWS
    r = lax.broadcasted_iota(jnp.int32, (seq, 1), 0)
    sign = jnp.where(((r // HALO_ROWS) & 1) == 0, 1.0, -1.0)
    xn = jnp.sum(zp_ref[...].astype(F32) * sign, axis=0, keepdims=True)
    yn_ref[...] = (xn * ny_ref[...] * (0.5 / seq)).reshape(yn_ref.shape)

    def emit(ch, re, im):
        kr = k_ref[0, ch * half:(ch + 1) * half, :].astype(F32)
        ki = k_ref[1, ch * half:(ch + 1) * half, :].astype(F32)
        y_ref[2 * ch * half:(2 * ch + 1) * half, :] = (re * kr - im * ki).astype(BF16)
        y_ref[(2 * ch + 1) * half:(2 * ch + 2) * half, :] = (re * ki + im * kr).astype(BF16)

    _fft_forward(zp_ref, f1_ref, mb_ref, a_scr, seq=seq, emit=emit)


def _hyena_fwd(zp, k, ny, tabs, *, batch, seq):
    width = zp.shape[1]
    w = FFT_LANES
    f1, mb = tabs["f1"], tabs["mb"]
    return pl.pallas_call(
        functools.partial(_hyena_fwd_kernel, seq=seq),
        grid=(batch, width // w),
        in_specs=[
            pl.BlockSpec((seq, w), lambda b, h: (b, h)),
            pl.BlockSpec(f1.shape, lambda b, h: (0, 0, 0)),
            pl.BlockSpec(mb.shape, lambda b, h: (0, 0)),
            pl.BlockSpec((2, seq, w), lambda b, h: (0, 0, h)),
            pl.BlockSpec((1, w), lambda b, h: (0, h)),
        ],
        out_specs=[
            pl.BlockSpec((2 * seq, w), lambda b, h: (b, h)),
            pl.BlockSpec((1, 1, w), lambda b, h: (b, 0, h)),
        ],
        out_shape=[
            jax.ShapeDtypeStruct((batch * 2 * seq, width), BF16),
            jax.ShapeDtypeStruct((batch, 1, width), F32),
        ],
        scratch_shapes=[pltpu.VMEM((FFT_B, f1.shape[1], w), F32)],
        compiler_params=_params("parallel", "parallel"),
        name="hyena_fwd",
    )(zp, f1, mb, k, ny)


def _hyena_inv_kernel(y_ref, mbi_ref, f1t_ref, yn_ref, zp_ref, x0p_ref, bd_ref, o_ref, g_scr, *, seq):
    t1 = 2 * seq // FFT_B
    r8 = F32_TILE_ROWS
    nch = t1 // r8
    blk_rows = 2 * FFT_D * r8
    mbi = mbi_ref[...]
    for ch in range(nch):
        g_scr[ch] = jnp.dot(mbi, y_ref[ch * blk_rows:(ch + 1) * blk_rows, :], preferred_element_type=F32)
    half = FFT_B * r8
    yn = yn_ref[0]
    bd = bd_ref[...]
    for b in range(FFT_B):
        re = [g_scr[ch, b * r8:(b + 1) * r8, :] for ch in range(nch)]
        im = [g_scr[ch, half + b * r8:half + (b + 1) * r8, :] for ch in range(nch)]
        yb = jnp.dot(f1t_ref[b], jnp.concatenate(re + im, axis=0).astype(BF16), preferred_element_type=F32)
        nyq = yn if b % 2 == 0 else -yn
        for blk in range(seq // PERM_ROWS):
            rows = slice(blk * PERM_ROWS + b * HALO_ROWS, blk * PERM_ROWS + (b + 1) * HALO_ROWS)
            conv = yb[blk * HALO_ROWS:(blk + 1) * HALO_ROWS, :]
            o_ref[rows, :] = x0p_ref[rows, :].astype(F32) * (conv + nyq + zp_ref[rows, :].astype(F32) * bd)


def _hyena_inv(y, yn, zp, x0p, bd, tabs, *, batch, seq):
    width = y.shape[1]
    w = FFT_LANES
    f1t, mbi = tabs["f1t"], tabs["mbi"]
    blk = pl.BlockSpec((seq, w), lambda b, h: (b, h))
    return pl.pallas_call(
        functools.partial(_hyena_inv_kernel, seq=seq),
        grid=(batch, width // w),
        in_specs=[
            pl.BlockSpec((2 * seq, w), lambda b, h: (b, h)),
            pl.BlockSpec(mbi.shape, lambda b, h: (0, 0)),
            pl.BlockSpec(f1t.shape, lambda b, h: (0, 0, 0)),
            pl.BlockSpec((1, 1, w), lambda b, h: (b, 0, h)),
            blk, blk,
            pl.BlockSpec((1, w), lambda b, h: (0, h)),
        ],
        out_specs=blk,
        out_shape=jax.ShapeDtypeStruct((batch * seq, width), F32),
        scratch_shapes=[pltpu.VMEM((2 * seq // FFT_B // F32_TILE_ROWS, 2 * FFT_B * F32_TILE_ROWS, w), F32)],
        compiler_params=_params("parallel", "parallel"),
        name="hyena_inv",
    )(y, mbi, f1t, yn, zp, x0p, bd)


HEADS_PER_TILE = LANES // HEAD_DIM
ATTN_GROUPS = tuple(
    tuple(h for h in range(N_HEADS)
          if (h // GROUP) // HEADS_PER_TILE == tile and (h % HEADS_PER_TILE != (h // GROUP) % HEADS_PER_TILE) == swapped)
    for tile in range(N_KV_HEADS // HEADS_PER_TILE) for swapped in (False, True))
ATTN_GROUP_ROWS = len(ATTN_GROUPS[0]) * BLOCK


def _attn_kernel(q_ref, kp_ref, kc_ref, kn_ref, vp_ref, vc_ref, vn_ref, qg_ref, kg_ref, sink_ref, o_ref, bias_ref,
                 *, blocks):
    step = pl.program_id(1)
    last_step = pl.num_programs(1) - 1
    keys = 3 * BLOCK
    grows = ATTN_GROUP_ROWS

    @pl.when((pl.program_id(0) == 0) & (step == 0))
    def _():
        qi = lax.broadcasted_iota(jnp.int32, (BLOCK, keys), 0)
        kj = lax.broadcasted_iota(jnp.int32, (BLOCK, keys), 1)
        dist = jnp.abs(qi - (kj - BLOCK))
        distf = dist.astype(F32)
        for variant, (lo, hi) in enumerate(((BLOCK, keys), (0, keys), (0, 2 * BLOCK))):
            ok = (dist <= WINDOW) & (kj >= lo) & (kj < hi)
            for g, heads in enumerate(ATTN_GROUPS):
                for j, h in enumerate(heads):
                    slope = 2.0 ** (-8.0 * (h + 1) / N_HEADS)
                    bias_ref[variant, g, j * BLOCK:(j + 1) * BLOCK, :] = jnp.where(ok, -slope * distf, NEG_INF)

    seg_r = lax.broadcasted_iota(jnp.int32, (D_KV, D_KV), 0) // HEAD_DIM
    seg_c = lax.broadcasted_iota(jnp.int32, (D_KV, D_KV), 1) // HEAD_DIM
    seg_mean = jnp.where(seg_r == seg_c, 1.0 / HEAD_DIM, 0.0).astype(BF16)

    def head_norm(x, g):
        ms = jnp.dot((x * x).astype(BF16), seg_mean, preferred_element_type=F32)
        return x * lax.rsqrt(ms + EPS) * g

    low = lax.broadcasted_iota(jnp.int32, (1, LANES), 1) < HEAD_DIM
    qg = jnp.concatenate([qg_ref[...]] * GROUP, axis=1) * (HEAD_DIM ** -0.5)
    kg = jnp.concatenate([kg_ref[...]] * N_KV_HEADS, axis=1)
    k = jnp.concatenate([kp_ref[...], kc_ref[...], kn_ref[...]], axis=0).astype(F32)
    k = head_norm(k, kg)
    v = jnp.concatenate([vp_ref[...], vc_ref[...], vn_ref[...]], axis=0).astype(F32)
    k_var, v_var = [], []
    for t in range(D_KV // LANES):
        kt = k[:, t * LANES:(t + 1) * LANES]
        vt = v[:, t * LANES:(t + 1) * LANES]
        k_var += [kt.astype(BF16), pltpu.roll(kt, HEAD_DIM, axis=1).astype(BF16)]
        v_var += [vt.astype(BF16), pltpu.roll(vt, HEAD_DIM, axis=1).astype(BF16)]

    q_half = {}
    for c in range(N_KV_HEADS):
        qn = head_norm(q_ref[:, c * D_KV:(c + 1) * D_KV].astype(F32), qg)
        for u in range(D_KV // LANES):
            qt = qn[:, u * LANES:(u + 1) * LANES]
            tile = c * (D_KV // LANES) + u
            q_half[tile, 0] = jnp.where(low, qt, 0.0).astype(BF16)
            q_half[tile, 1] = jnp.where(low, 0.0, qt).astype(BF16)

    row_head = lax.broadcasted_iota(jnp.int32, (grows, 1), 0) // BLOCK
    sinks = []
    for heads in ATTN_GROUPS:
        col = jnp.zeros((grows, 1), F32)
        for j, h in enumerate(heads):
            col = jnp.where(row_head == j, sink_ref[0:1, h:h + 1], col)
        sinks.append(col)

    for qb in range(blocks):
        rows = slice(qb * BLOCK, (qb + 1) * BLOCK)
        krows = slice(qb * BLOCK, qb * BLOCK + keys)
        if qb == 0:
            variant = jnp.where(step == 0, 0, 1)
        elif qb == blocks - 1:
            variant = jnp.where(step == last_step, 2, 1)
        else:
            variant = 1
        res = {}
        for g, heads in enumerate(ATTN_GROUPS):
            qs = jnp.concatenate([q_half[h // HEADS_PER_TILE, h % HEADS_PER_TILE][rows] for h in heads], axis=0)
            s = lax.dot_general(qs, k_var[g][krows], (((1,), (1,)), ((), ())), preferred_element_type=F32)
            logits = s + bias_ref[variant, g]
            m = jnp.maximum(jnp.max(logits, axis=-1, keepdims=True), sinks[g])
            p = jnp.exp(logits - m)
            denom = jnp.sum(p, axis=-1, keepdims=True) + jnp.exp(sinks[g] - m)
            o = jnp.dot(p.astype(BF16), v_var[g][krows], preferred_element_type=F32) / denom
            for j, h in enumerate(heads):
                res[h] = o[j * BLOCK:(j + 1) * BLOCK]
        for tile in range(N_HEADS // HEADS_PER_TILE):
            o_ref[rows, tile * LANES:(tile + 1) * LANES] = jnp.where(
                low, res[HEADS_PER_TILE * tile], res[HEADS_PER_TILE * tile + 1])


def _attention(p, qg, kg, sink, *, batch, seq, blocks):
    n = p.shape[0]
    nb = seq // BLOCK
    tq = blocks * BLOCK
    ns = seq // tq
    assert HEADS_PER_TILE == 2 and blocks >= 2 and seq % tq == 0
    q_col = (3 * D_HYENA + 3 * D_SCONV) // D_ATTN
    k_col = (3 * D_HYENA + 3 * D_SCONV + D_ATTN) // D_KV
    v_col = k_col + 1
    assert q_col * D_ATTN == 3 * D_HYENA + 3 * D_SCONV and k_col * D_KV == q_col * D_ATTN + D_ATTN

    def kv_specs(col):
        prev = pl.BlockSpec((BLOCK, D_KV), lambda b, i: (b * nb + jnp.maximum(i * blocks - 1, 0), col))
        cur = pl.BlockSpec((tq, D_KV), lambda b, i: (b * ns + i, col))
        nxt = pl.BlockSpec((BLOCK, D_KV), lambda b, i: (b * nb + jnp.minimum((i + 1) * blocks, nb - 1), col))
        return [prev, cur, nxt]

    return pl.pallas_call(
        functools.partial(_attn_kernel, blocks=blocks),
        grid=(batch, ns),
        in_specs=[pl.BlockSpec((tq, D_ATTN), lambda b, i: (b * ns + i, q_col))] + kv_specs(k_col) + kv_specs(v_col) + [
            pl.BlockSpec((1, HEAD_DIM), lambda b, i: (0, 0)),
            pl.BlockSpec((1, HEAD_DIM), lambda b, i: (0, 0)),
            pl.BlockSpec((1, N_HEADS), lambda b, i: (0, 0)),
        ],
        out_specs=pl.BlockSpec((tq, D_ATTN), lambda b, i: (b * ns + i, 0)),
        out_shape=jax.ShapeDtypeStruct((n, D_ATTN), F32),
        scratch_shapes=[pltpu.VMEM((3, len(ATTN_GROUPS), ATTN_GROUP_ROWS, 3 * BLOCK), F32)],
        compiler_params=_params("arbitrary", "arbitrary"),
        name="attention",
    )(p, p, p, p, p, p, p, qg, kg, sink)


def _outproj_kernel(x_ref, yh_ref, ys_ref, ya_ref, g_ref, w_ref, unperm_ref, o_ref):
    d1 = D_HYENA
    d2 = D_HYENA + D_SCONV
    nh = _rms(yh_ref[...], g_ref[:, :d1]).astype(BF16)
    nh = jnp.dot(unperm_ref[...], nh, preferred_element_type=F32).astype(BF16)
    ns = _rms(ys_ref[...], g_ref[:, d1:d2]).astype(BF16)
    na = _rms(ya_ref[...], g_ref[:, d2:]).astype(BF16)
    acc = jnp.dot(nh, w_ref[:d1, :], preferred_element_type=F32)
    acc += jnp.dot(ns, w_ref[d1:d2, :], preferred_element_type=F32)
    acc += jnp.dot(na, w_ref[d2:, :], preferred_element_type=F32)
    o_ref[...] = x_ref[...] + acc


def _outproj(x, yh, ys, ya, g, w, unperm, *, layer):
    n, d = x.shape
    tm = PERM_ROWS
    row = lambda i: (i, 0)
    return pl.pallas_call(
        _outproj_kernel,
        grid=(n // tm,),
        in_specs=[
            pl.BlockSpec((tm, d), row),
            pl.BlockSpec((tm, D_HYENA), row),
            pl.BlockSpec((tm, D_SCONV), row),
            pl.BlockSpec((tm, D_ATTN), row),
            pl.BlockSpec((1, w.shape[1]), lambda i: (0, 0)),
            pl.BlockSpec((None,) + w.shape[1:], lambda i: (layer, 0, 0)),
            pl.BlockSpec(unperm.shape, lambda i: (0, 0)),
        ],
        out_specs=pl.BlockSpec((tm, d), row),
        out_shape=jax.ShapeDtypeStruct((n, d), F32),
        compiler_params=_params("parallel"),
        name="outproj",
    )(x, yh, ys, ya, g, w, unperm)


def _tile(n, want):
    t = min(n, want)
    assert n % t == 0
    return t


def _filter_features(seq):
    n = jnp.arange(seq, dtype=F32)
    t = n / float(max(seq - 1, 1))
    bands = jnp.linspace(1e-4, FILTER_BANDS - 1, FILTER_BANDS, dtype=F32)
    ang = (2.0 * math.pi / seq) * n[:, None] * bands[None, :]
    z = jnp.concatenate([t[:, None], jnp.cos(ang), jnp.sin(ang)], axis=-1)
    return jnp.pad(z, ((0, 0), (0, LANES - z.shape[1])))


def kernel(x, norm_ffn1, ffn1_w_gate, ffn1_w_up, ffn1_w_down, norm_mix, w_in, hyena_short_w, hyena_short_b,
           filt_w1, filt_b1, filt_w2, filt_b2, filt_w3, filt_b3, filt_freq, filt_w_out, hyena_bias, sconv_w,
           q_norm_g, k_norm_g, attn_sink, mix_out_g, w_out, norm_ffn2, ffn2_w_gate, ffn2_w_up, ffn2_w_down):
    batch, seq, d = x.shape
    depth = w_in.shape[0]
    n = batch * seq
    assert seq % PERM_ROWS == 0
    tm = _tile(n, 1024)
    tf = _tile(ffn1_w_gate.shape[2], 512)
    tn = _tile(w_in.shape[2], 1536)

    xf = x.reshape(n, d)
    row = lambda a: a.reshape(1, -1)
    tabs = _fft_tables(seq)
    z = _filter_features(seq)
    deltas = jnp.abs(jnp.linspace(HYENA_MIN_DECAY, HYENA_MAX_DECAY, D_HYENA, dtype=F32)).reshape(1, -1)

    w_in_b, w_out_b = _cast_bf16(w_in), _cast_bf16(w_out)
    ffn1_f32 = (ffn1_w_gate, ffn1_w_up, ffn1_w_down)
    ffn2_f32 = (ffn2_w_gate, ffn2_w_up, ffn2_w_down)
    ffn_w = tuple(_cast_bf16(w, 0) for w in ffn1_f32)

    for l in range(depth):
        xf, ffn_w = _ffn(xf, row(norm_ffn1[l]), *ffn_w, tm=tm, tf=tf, cast_next=ffn2_f32 + (l,))

        p = _inproj(xf, row(norm_mix[l]), w_in_b, layer=l, tm=tm, tn=tn)

        w1 = jnp.pad(filt_w1[l], ((0, z.shape[1] - filt_w1.shape[1]), (0, 0)))
        kpm, ny = _hyena_filter(z, w1, row(filt_b1[l]), filt_w2[l], row(filt_b2[l]), filt_w3[l], row(filt_b3[l]),
                                row(filt_freq[l]), filt_w_out[l], deltas, tabs["perm"])
        kf = _filter_fft(kpm, tabs)

        zp, x0p, y_sc = _mixprep(p, hyena_short_w[l], row(hyena_short_b[l]), sconv_w[l], tabs["perm"],
                                 batch=batch, seq=seq)
        y, yn = _hyena_fwd(zp, kf, ny, tabs, batch=batch, seq=seq)
        y_hy = _hyena_inv(y, yn, zp, x0p, row(hyena_bias[l]), tabs, batch=batch, seq=seq)

        y_at = _attention(p, row(q_norm_g[l]), row(k_norm_g[l]), row(attn_sink[l]), batch=batch, seq=seq,
                          blocks=_tile(seq // BLOCK, 8))

        xf = _outproj(xf, y_hy, y_sc, y_at, row(mix_out_g[l]), w_out_b, tabs["unperm"], layer=l)

        cast_next = ffn1_f32 + (l + 1,) if l + 1 < depth else None
        xf, ffn_w = _ffn(xf, row(norm_ffn2[l]), *ffn_w, tm=tm, tf=tf, cast_next=cast_next)

    return xf.reshape(batch, seq, d)
```

```python
import functools
import math

import jax
import jax.numpy as jnp
from jax import lax
from jax.experimental import pallas as pl
from jax.experimental.pallas import tpu as pltpu

F32 = jnp.float32
BF16 = jnp.bfloat16

EPS = 1e-6
NEG_INF = -1e30

D_HYENA = 512
D_SCONV = 512
N_HEADS = 16
N_KV_HEADS = 4
GROUP = N_HEADS // N_KV_HEADS
HEAD_DIM = 64
D_ATTN = N_HEADS * HEAD_DIM
D_KV = N_KV_HEADS * HEAD_DIM
WINDOW = 128
BLOCK = 128
FILTER_BANDS = 16
FILTER_HIDDEN = 64
HYENA_MIN_DECAY = math.log(1e-2) / 1.5
HYENA_MAX_DECAY = math.log(1e-2) / 0.3
HYENA_SHIFT = 0.05

V7X_VMEM_LIMIT_BYTES = 60 * 1024 * 1024
HALO_ROWS = 16
LANES = 128
F32_TILE_ROWS = 8

FFT_B = 32
FFT_D = FFT_B // 2
PERM_ROWS = FFT_B * HALO_ROWS
FFT_LANES = 256


def _params(*sem):
    return pltpu.CompilerParams(dimension_semantics=sem, vmem_limit_bytes=V7X_VMEM_LIMIT_BYTES)


def _rms(x, g):
    ms = jnp.mean(x * x, axis=-1, keepdims=True)
    return x * lax.rsqrt(ms + EPS) * g


CAST_BLOCK_BYTES = 6 * 1024 * 1024


def _cast_kernel(w_ref, o_ref):
    o_ref[...] = w_ref[...].astype(BF16)


def _cast_bf16(w, layer):
    _, r, c = w.shape
    rows = r
    while rows * c * 4 > CAST_BLOCK_BYTES and rows % 32 == 0:
        rows //= 2
    return pl.pallas_call(
        _cast_kernel,
        grid=(r // rows,),
        in_specs=[pl.BlockSpec((None, rows, c), lambda i: (layer, i, 0))],
        out_specs=pl.BlockSpec((rows, c), lambda i: (i, 0)),
        out_shape=jax.ShapeDtypeStruct((r, c), BF16),
        compiler_params=_params("parallel"),
        name="cast_bf16",
    )(w)


def _ffn_kernel(x_ref, g_ref, wg_ref, wu_ref, wd_ref, *rest, n_side):
    side_in, o_ref, side_out, h_ref = rest[:n_side], rest[n_side], rest[n_side + 1:2 * n_side + 1], rest[-1]

    @pl.when(pl.program_id(1) == 0)
    def _():
        x = x_ref[...]
        h_ref[...] = _rms(x, g_ref[...]).astype(BF16)
        o_ref[...] = x

    h = h_ref[...]
    gate = jnp.dot(h, wg_ref[...], preferred_element_type=F32)
    up = jnp.dot(h, wu_ref[...], preferred_element_type=F32)
    a = (gate * jax.nn.sigmoid(gate) * up * 0.5).astype(BF16)
    o_ref[...] += jnp.dot(a, wd_ref[...], preferred_element_type=F32)
    for src, dst in zip(side_in, side_out):
        dst[...] = src[...].astype(BF16)


def _cast_blocks(shape, ni, nj):
    r, c = shape
    br = r // ni
    assert r % ni == 0 and br % HALO_ROWS == 0
    ncb = nj
    while c % ncb or (c // ncb) % LANES:
        ncb -= 1
    return (br, c // ncb), (lambda i, j: (i, jnp.minimum(j, ncb - 1)))


def _ffn(x, g, wg, wu, wd, *, tm, tf, cast=()):
    n, d = x.shape
    ff = wg.shape[1]
    ni, nj = n // tm, ff // tf
    side_args, side_in_specs, side_out_specs, side_shapes = [], [], [], []
    for w, layer in cast:
        blk, idx = _cast_blocks(w.shape[1:], ni, nj)
        side_args.append(w)
        side_in_specs.append(pl.BlockSpec((None,) + blk, lambda i, j, idx=idx, layer=layer: (layer,) + idx(i, j)))
        side_out_specs.append(pl.BlockSpec(blk, idx))
        side_shapes.append(jax.ShapeDtypeStruct(w.shape[1:], BF16))
    out = pl.pallas_call(
        functools.partial(_ffn_kernel, n_side=len(side_args)),
        grid=(ni, nj),
        in_specs=[
            pl.BlockSpec((tm, d), lambda i, j: (i, 0)),
            pl.BlockSpec((1, d), lambda i, j: (0, 0)),
            pl.BlockSpec((d, tf), lambda i, j: (0, j)),
            pl.BlockSpec((d, tf), lambda i, j: (0, j)),
            pl.BlockSpec((tf, d), lambda i, j: (j, 0)),
        ] + side_in_specs,
        out_specs=[pl.BlockSpec((tm, d), lambda i, j: (i, 0))] + side_out_specs,
        out_shape=[jax.ShapeDtypeStruct((n, d), F32)] + side_shapes,
        scratch_shapes=[pltpu.VMEM((tm, d), BF16)],
        compiler_params=_params("parallel", "arbitrary"),
        name="ffn",
    )(x, g, wg, wu, wd, *side_args)
    return out[0], tuple(out[1:])


def _inproj_kernel(x_ref, g_ref, w_ref, o_ref, h_ref):
    @pl.when(pl.program_id(1) == 0)
    def _():
        h_ref[...] = _rms(x_ref[...], g_ref[...]).astype(BF16)

    o_ref[...] = jnp.dot(h_ref[...], w_ref[...], preferred_element_type=F32).astype(o_ref.dtype)


def _inproj(x, g, w, *, tm, tn):
    n, d = x.shape
    dp = w.shape[1]
    return pl.pallas_call(
        _inproj_kernel,
        grid=(n // tm, dp // tn),
        in_specs=[
            pl.BlockSpec((tm, d), lambda i, j: (i, 0)),
            pl.BlockSpec((1, d), lambda i, j: (0, 0)),
            pl.BlockSpec((d, tn), lambda i, j: (0, j)),
        ],
        out_specs=pl.BlockSpec((tm, tn), lambda i, j: (i, j)),
        out_shape=jax.ShapeDtypeStruct((n, dp), BF16),
        scratch_shapes=[pltpu.VMEM((tm, d), BF16)],
        compiler_params=_params("parallel", "arbitrary"),
        name="inproj",
    )(x, g, w)


def _fft_tables(seq):
    period = 2 * seq
    t1 = period // FFT_B
    na = seq // FFT_B
    b = jnp.arange(FFT_B, dtype=jnp.int32)[:, None, None]
    c = jnp.arange(t1, dtype=jnp.int32)[None, :, None]
    a = jnp.arange(na, dtype=jnp.int32)[None, None, :]
    ang_a = ((c * a) % t1).astype(F32) * (2.0 * math.pi / t1)
    ang_b = ((c * b) % period).astype(F32) * (2.0 * math.pi / period)
    cos_a, sin_a, cos_b, sin_b = jnp.cos(ang_a), jnp.sin(ang_a), jnp.cos(ang_b), jnp.sin(ang_b)
    f1 = jnp.concatenate([cos_a * cos_b - sin_a * sin_b, -(sin_a * cos_b + cos_a * sin_b)], axis=1)
    d = jnp.arange(FFT_D, dtype=jnp.int32)[:, None]
    bb = jnp.arange(FFT_B, dtype=jnp.int32)[None, :]
    ang2 = ((d * bb) % FFT_B).astype(F32) * (2.0 * math.pi / FFT_B)
    eye = jnp.eye(F32_TILE_ROWS, dtype=F32)
    cr = jnp.kron(jnp.cos(ang2), eye)
    sr = jnp.kron(jnp.sin(ang2), eye)
    mb = jnp.block([[cr, sr], [-sr, cr]])
    mbi = jnp.block([[cr.T, -sr.T], [sr.T, cr.T]])
    r = jnp.arange(PERM_ROWS, dtype=jnp.int32)
    src = (r % HALO_ROWS) * FFT_B + r // HALO_ROWS
    perm = (src[:, None] == r[None, :]).astype(BF16)
    return dict(f1=f1.astype(BF16), f1t=jnp.swapaxes(f1, 1, 2).astype(BF16), mb=mb.astype(BF16),
                mbi=mbi.astype(BF16), perm=perm, unperm=perm.T)


def _filter_kernel(z_ref, w1_ref, b1_ref, w2_ref, b2_ref, w3_ref, b3_ref, fr_ref, wo_ref, dl_ref, perm_ref,
                   kpm_ref, ny_ref, *, seq, rows):
    i = pl.program_id(0)
    hp = lax.Precision.HIGHEST
    fr = fr_ref[...]
    h = jnp.sin(fr * (jnp.dot(z_ref[...], w1_ref[...], precision=hp, preferred_element_type=F32) + b1_ref[...]))
    h = jnp.sin(fr * (jnp.dot(h, w2_ref[...], precision=hp, preferred_element_type=F32) + b2_ref[...]))
    h = jnp.sin(fr * (jnp.dot(h, w3_ref[...], precision=hp, preferred_element_type=F32) + b3_ref[...]))
    h = jnp.dot(h, wo_ref[...], precision=hp, preferred_element_type=F32)
    n = lax.broadcasted_iota(jnp.int32, (rows, 1), 0) + i * rows
    t = n.astype(F32) / float(max(seq - 1, 1))
    window = jnp.exp(-t * dl_ref[...]) + HYENA_SHIFT
    hf = h[:, :D_HYENA] * window
    hb = jnp.where(n == 0, 0.0, h[:, D_HYENA:] * window)
    kp = hf + hb
    perm = perm_ref[...]
    kpm_ref[0] = jnp.dot(perm, kp.astype(BF16), preferred_element_type=F32).astype(BF16)
    kpm_ref[1] = jnp.dot(perm, (hf - hb).astype(BF16), preferred_element_type=F32).astype(BF16)
    sign = jnp.where((n & 1) == 0, 1.0, -1.0)

    @pl.when(i == 0)
    def _():
        ny_ref[...] = jnp.zeros_like(ny_ref)

    ny_ref[...] += jnp.sum(kp * sign, axis=0, keepdims=True)


def _hyena_filter(z, w1, b1, w2, b2, w3, b3, fr, wo, deltas, perm):
    seq, zw = z.shape
    rows = PERM_ROWS
    full = lambda a: pl.BlockSpec(a.shape, lambda i: (0,) * a.ndim)
    return pl.pallas_call(
        functools.partial(_filter_kernel, seq=seq, rows=rows),
        grid=(seq // rows,),
        in_specs=[pl.BlockSpec((rows, zw), lambda i: (i, 0))] + [
            full(a) for a in (w1, b1, w2, b2, w3, b3, fr, wo, deltas, perm)],
        out_specs=[
            pl.BlockSpec((2, rows, D_HYENA), lambda i: (0, i, 0)),
            pl.BlockSpec((1, D_HYENA), lambda i: (0, 0)),
        ],
        out_shape=[
            jax.ShapeDtypeStruct((2, seq, D_HYENA), BF16),
            jax.ShapeDtypeStruct((1, D_HYENA), F32),
        ],
        compiler_params=_params("arbitrary"),
        name="hyena_filter",
    )(z, w1, b1, w2, b2, w3, b3, fr, wo, deltas, perm)


def _shift_rows(u, halo_prev, halo_next, first, last):
    rows = u.shape[0]
    r = lax.broadcasted_iota(jnp.int32, (rows, 1), 0)
    prev_row = jnp.where(first, 0.0, halo_prev)
    next_row = jnp.where(last, 0.0, halo_next)
    um = jnp.where(r == 0, prev_row, pltpu.roll(u, 1, axis=0))
    up = jnp.where(r == rows - 1, next_row, pltpu.roll(u, rows - 1, axis=0))
    return um, up


def _mixprep_kernel(p_ref, pp_ref, pn_ref, hw_ref, hb_ref, sw_ref, perm_ref, zp_ref, x0p_ref, ysc_ref):
    first = pl.program_id(1) == 0
    last = pl.program_id(1) == pl.num_programs(1) - 1
    nh = 3 * D_HYENA
    hr = HALO_ROWS
    p = p_ref[:, :nh].astype(F32)
    pm, pp = _shift_rows(p, pp_ref[hr - 1:hr, :nh].astype(F32), pn_ref[0:1, :nh].astype(F32), first, last)
    u = pm * hw_ref[0:1, :] + p * hw_ref[1:2, :] + pp * hw_ref[2:3, :] + hb_ref[...]
    zv = u[:, 2 * D_HYENA:] * u[:, D_HYENA:2 * D_HYENA]
    perm = perm_ref[...]
    zp_ref[...] = jnp.dot(perm, zv.astype(BF16), preferred_element_type=F32).astype(BF16)
    x0p_ref[...] = jnp.dot(perm, u[:, :D_HYENA].astype(BF16), preferred_element_type=F32).astype(BF16)
    ds = D_SCONV
    gb = p_ref[:, nh:nh + ds].astype(F32)
    ch = p_ref[:, nh + ds:nh + 2 * ds].astype(F32) * p_ref[:, nh + 2 * ds:nh + 3 * ds].astype(F32)
    ch_prev = pp_ref[hr - 1:hr, nh + ds:nh + 2 * ds].astype(F32) * pp_ref[hr - 1:hr, nh + 2 * ds:nh + 3 * ds].astype(F32)
    ch_next = pn_ref[0:1, nh + ds:nh + 2 * ds].astype(F32) * pn_ref[0:1, nh + 2 * ds:nh + 3 * ds].astype(F32)
    cm, cp = _shift_rows(ch, ch_prev, ch_next, first, last)
    ysc_ref[...] = gb * (cm * sw_ref[0:1, :] + ch * sw_ref[1:2, :] + cp * sw_ref[2:3, :])


def _mixprep(p, hw, hb, sw, perm, *, batch, seq):
    n = batch * seq
    tl = PERM_ROWS
    nl = seq // tl
    width = 3 * D_HYENA + 3 * D_SCONV
    hb8 = tl // HALO_ROWS
    last8 = seq // HALO_ROWS - 1
    row = lambda b, l: (b * nl + l, 0)
    const = lambda a: pl.BlockSpec(a.shape, lambda b, l: (0, 0))
    return pl.pallas_call(
        _mixprep_kernel,
        grid=(batch, nl),
        in_specs=[
            pl.BlockSpec((tl, width), row),
            pl.BlockSpec((HALO_ROWS, width), lambda b, l: (b * (last8 + 1) + jnp.maximum(l * hb8 - 1, 0), 0)),
            pl.BlockSpec((HALO_ROWS, width), lambda b, l: (b * (last8 + 1) + jnp.minimum((l + 1) * hb8, last8), 0)),
            const(hw), const(hb), const(sw), const(perm),
        ],
        out_specs=[pl.BlockSpec((tl, D_HYENA), row)] * 3,
        out_shape=[
            jax.ShapeDtypeStruct((n, D_HYENA), BF16),
            jax.ShapeDtypeStruct((n, D_HYENA), BF16),
            jax.ShapeDtypeStruct((n, D_SCONV), F32),
        ],
        compiler_params=_params("parallel", "parallel"),
        name="mixprep",
    )(p, p, p, hw, hb, sw, perm)


def _fft_forward(zp_ref, f1_ref, mb_ref, a_scr, *, seq, emit):
    t1 = 2 * seq // FFT_B
    r8 = F32_TILE_ROWS
    for b in range(FFT_B):
        xb = jnp.concatenate([zp_ref[blk * PERM_ROWS + b * HALO_ROWS:blk * PERM_ROWS + (b + 1) * HALO_ROWS, :]
                              for blk in range(seq // PERM_ROWS)], axis=0)
        a_scr[b] = jnp.dot(f1_ref[b], xb, preferred_element_type=F32)
    mb = mb_ref[...]
    half = FFT_D * r8
    for ch in range(t1 // r8):
        re = [a_scr[b, ch * r8:(ch + 1) * r8, :] for b in range(FFT_B)]
        im = [a_scr[b, t1 + ch * r8:t1 + (ch + 1) * r8, :] for b in range(FFT_B)]
        xs = jnp.dot(mb, jnp.concatenate(re + im, axis=0).astype(BF16), preferred_element_type=F32)
        emit(ch, xs[:half], xs[half:])


def _filter_fft_kernel(kpm_ref, f1_ref, mb_ref, k_ref, a_scr, *, seq):
    part = pl.program_id(0)
    half = FFT_D * F32_TILE_ROWS

    def emit(ch, re, im):
        val = jnp.where(part == 0, re, im) * (1.0 / seq)
        if ch == 0:
            val = jnp.where(lax.broadcasted_iota(jnp.int32, (half, 1), 0) == 0, 0.5 * val, val)
        k_ref[ch * half:(ch + 1) * half, :] = val.astype(BF16)

    _fft_forward(kpm_ref, f1_ref, mb_ref, a_scr, seq=seq, emit=emit)


def _filter_fft(kpm, tabs):
    _, seq, width = kpm.shape
    w = FFT_LANES
    f1, mb = tabs["f1"], tabs["mb"]
    return pl.pallas_call(
        functools.partial(_filter_fft_kernel, seq=seq),
        grid=(2, width // w),
        in_specs=[
            pl.BlockSpec((None, seq, w), lambda p, h: (p, 0, h)),
            pl.BlockSpec(f1.shape, lambda p, h: (0, 0, 0)),
            pl.BlockSpec(mb.shape, lambda p, h: (0, 0)),
        ],
        out_specs=pl.BlockSpec((None, seq, w), lambda p, h: (p, 0, h)),
        out_shape=jax.ShapeDtypeStruct((2, seq, width), BF16),
        scratch_shapes=[pltpu.VMEM((FFT_B, f1.shape[1], w), F32)],
        compiler_params=_params("parallel", "parallel"),
        name="filter_fft",
    )(kpm, f1, mb)


def _hyena_fwd_kernel(zp_ref, f1_ref, mb_ref, k_ref, ny_ref, y_ref, yn_ref, a_scr, *, seq):
    half = FFT_D * F32_TILE_ROWS
    r = lax.broadcasted_iota(jnp.int32, (seq, 1), 0)
    sign = jnp.where(((r // HALO_ROWS) & 1) == 0, 1.0, -1.0)
    xn = jnp.sum(zp_ref[...].astype(F32) * sign, axis=0, keepdims=True)
    yn_ref[...] = (xn * ny_ref[...] * (0.5 / seq)).reshape(yn_ref.shape)

    def emit(ch, re, im):
        kr = k_ref[0, ch * half:(ch + 1) * half, :].astype(F32)
        ki = k_ref[1, ch * half:(ch + 1) * half, :].astype(F32)
        y_ref[2 * ch * half:(2 * ch + 1) * half, :] = (re * kr - im * ki).astype(BF16)
        y_ref[(2 * ch + 1) * half:(2 * ch + 2) * half, :] = (re * ki + im * kr).astype(BF16)

    _fft_forward(zp_ref, f1_ref, mb_ref, a_scr, seq=seq, emit=emit)


def _hyena_fwd(zp, k, ny, tabs, *, batch, seq):
    width = zp.shape[1]
    w = FFT_LANES
    f1, mb = tabs["f1"], tabs["mb"]
    return pl.pallas_call(
        functools.partial(_hyena_fwd_kernel, seq=seq),
        grid=(batch, width // w),
        in_specs=[
            pl.BlockSpec((seq, w), lambda b, h: (b, h)),
            pl.BlockSpec(f1.shape, lambda b, h: (0, 0, 0)),
            pl.BlockSpec(mb.shape, lambda b, h: (0, 0)),
            pl.BlockSpec((2, seq, w), lambda b, h: (0, 0, h)),
            pl.BlockSpec((1, w), lambda b, h: (0, h)),
        ],
        out_specs=[
            pl.BlockSpec((2 * seq, w), lambda b, h: (b, h)),
            pl.BlockSpec((1, 1, w), lambda b, h: (b, 0, h)),
        ],
        out_shape=[
            jax.ShapeDtypeStruct((batch * 2 * seq, width), BF16),
            jax.ShapeDtypeStruct((batch, 1, width), F32),
        ],
        scratch_shapes=[pltpu.VMEM((FFT_B, f1.shape[1], w), F32)],
        compiler_params=_params("parallel", "parallel"),
        name="hyena_fwd",
    )(zp, f1, mb, k, ny)


def _hyena_inv_kernel(y_ref, mbi_ref, f1t_ref, yn_ref, zp_ref, x0p_ref, bd_ref, o_ref, g_scr, *, seq):
    t1 = 2 * seq // FFT_B
    r8 = F32_TILE_ROWS
    nch = t1 // r8
    blk_rows = 2 * FFT_D * r8
    mbi = mbi_ref[...]
    for ch in range(nch):
        g_scr[ch] = jnp.dot(mbi, y_ref[ch * blk_rows:(ch + 1) * blk_rows, :], preferred_element_type=F32)
    half = FFT_B * r8
    yn = yn_ref[0]
    bd = bd_ref[...]
    for b in range(FFT_B):
        re = [g_scr[ch, b * r8:(b + 1) * r8, :] for ch in range(nch)]
        im = [g_scr[ch, half + b * r8:half + (b + 1) * r8, :] for ch in range(nch)]
        yb = jnp.dot(f1t_ref[b], jnp.concatenate(re + im, axis=0).astype(BF16), preferred_element_type=F32)
        nyq = yn if b % 2 == 0 else -yn
        for blk in range(seq // PERM_ROWS):
            rows = slice(blk * PERM_ROWS + b * HALO_ROWS, blk * PERM_ROWS + (b + 1) * HALO_ROWS)
            conv = yb[blk * HALO_ROWS:(blk + 1) * HALO_ROWS, :]
            o_ref[rows, :] = x0p_ref[rows, :].astype(F32) * (conv + nyq + zp_ref[rows, :].astype(F32) * bd)


def _hyena_inv(y, yn, zp, x0p, bd, tabs, *, batch, seq):
    width = y.shape[1]
    w = FFT_LANES
    f1t, mbi = tabs["f1t"], tabs["mbi"]
    blk = pl.BlockSpec((seq, w), lambda b, h: (b, h))
    return pl.pallas_call(
        functools.partial(_hyena_inv_kernel, seq=seq),
        grid=(batch, width // w),
        in_specs=[
            pl.BlockSpec((2 * seq, w), lambda b, h: (b, h)),
            pl.BlockSpec(mbi.shape, lambda b, h: (0, 0)),
            pl.BlockSpec(f1t.shape, lambda b, h: (0, 0, 0)),
            pl.BlockSpec((1, 1, w), lambda b, h: (b, 0, h)),
            blk, blk,
            pl.BlockSpec((1, w), lambda b, h: (0, h)),
        ],
        out_specs=blk,
        out_shape=jax.ShapeDtypeStruct((batch * seq, width), F32),
        scratch_shapes=[pltpu.VMEM((2 * seq // FFT_B // F32_TILE_ROWS, 2 * FFT_B * F32_TILE_ROWS, w), F32)],
        compiler_params=_params("parallel", "parallel"),
        name="hyena_inv",
    )(y, mbi, f1t, yn, zp, x0p, bd)


HEADS_PER_TILE = LANES // HEAD_DIM
ATTN_GROUPS = tuple(
    tuple(h for h in range(N_HEADS)
          if (h // GROUP) // HEADS_PER_TILE == tile and (h % HEADS_PER_TILE != (h // GROUP) % HEADS_PER_TILE) == swapped)
    for tile in range(N_KV_HEADS // HEADS_PER_TILE) for swapped in (False, True))
ATTN_GROUP_ROWS = len(ATTN_GROUPS[0]) * BLOCK


def _attn_kernel(q_ref, kp_ref, kc_ref, kn_ref, vp_ref, vc_ref, vn_ref, qg_ref, kg_ref, sink_ref, o_ref, bias_ref,
                 *, blocks):
    step = pl.program_id(1)
    last_step = pl.num_programs(1) - 1
    keys = 3 * BLOCK
    grows = ATTN_GROUP_ROWS

    @pl.when((pl.program_id(0) == 0) & (step == 0))
    def _():
        qi = lax.broadcasted_iota(jnp.int32, (BLOCK, keys), 0)
        kj = lax.broadcasted_iota(jnp.int32, (BLOCK, keys), 1)
        dist = jnp.abs(qi - (kj - BLOCK))
        distf = dist.astype(F32)
        for variant, (lo, hi) in enumerate(((BLOCK, keys), (0, keys), (0, 2 * BLOCK))):
            ok = (dist <= WINDOW) & (kj >= lo) & (kj < hi)
            for g, heads in enumerate(ATTN_GROUPS):
                for j, h in enumerate(heads):
                    slope = 2.0 ** (-8.0 * (h + 1) / N_HEADS)
                    bias_ref[variant, g, j * BLOCK:(j + 1) * BLOCK, :] = jnp.where(ok, -slope * distf, NEG_INF)

    seg_r = lax.broadcasted_iota(jnp.int32, (D_KV, D_KV), 0) // HEAD_DIM
    seg_c = lax.broadcasted_iota(jnp.int32, (D_KV, D_KV), 1) // HEAD_DIM
    seg_mean = jnp.where(seg_r == seg_c, 1.0 / HEAD_DIM, 0.0).astype(BF16)

    def head_norm(x, g):
        ms = jnp.dot((x * x).astype(BF16), seg_mean, preferred_element_type=F32)
        return x * lax.rsqrt(ms + EPS) * g

    low = lax.broadcasted_iota(jnp.int32, (1, LANES), 1) < HEAD_DIM
    qg = jnp.concatenate([qg_ref[...]] * GROUP, axis=1) * (HEAD_DIM ** -0.5)
    kg = jnp.concatenate([kg_ref[...]] * N_KV_HEADS, axis=1)
    k = jnp.concatenate([kp_ref[...], kc_ref[...], kn_ref[...]], axis=0).astype(F32)
    k = head_norm(k, kg)
    v = jnp.concatenate([vp_ref[...], vc_ref[...], vn_ref[...]], axis=0).astype(F32)
    k_var, v_var = [], []
    for t in range(D_KV // LANES):
        kt = k[:, t * LANES:(t + 1) * LANES]
        vt = v[:, t * LANES:(t + 1) * LANES]
        k_var += [kt.astype(BF16), pltpu.roll(kt, HEAD_DIM, axis=1).astype(BF16)]
        v_var += [vt.astype(BF16), pltpu.roll(vt, HEAD_DIM, axis=1).astype(BF16)]

    q_half = {}
    for c in range(N_KV_HEADS):
        qn = head_norm(q_ref[:, c * D_KV:(c + 1) * D_KV].astype(F32), qg)
        for u in range(D_KV // LANES):
            qt = qn[:, u * LANES:(u + 1) * LANES]
            tile = c * (D_KV // LANES) + u
            q_half[tile, 0] = jnp.where(low, qt, 0.0).astype(BF16)
            q_half[tile, 1] = jnp.where(low, 0.0, qt).astype(BF16)

    row_head = lax.broadcasted_iota(jnp.int32, (grows, 1), 0) // BLOCK
    sinks = []
    for heads in ATTN_GROUPS:
        col = jnp.zeros((grows, 1), F32)
        for j, h in enumerate(heads):
            col = jnp.where(row_head == j, sink_ref[0:1, h:h + 1], col)
        sinks.append(col)

    for qb in range(blocks):
        rows = slice(qb * BLOCK, (qb + 1) * BLOCK)
        krows = slice(qb * BLOCK, qb * BLOCK + keys)
        if qb == 0:
            variant = jnp.where(step == 0, 0, 1)
        elif qb == blocks - 1:
            variant = jnp.where(step == last_step, 2, 1)
        else:
            variant = 1
        res = {}
        for g, heads in enumerate(ATTN_GROUPS):
            qs = jnp.concatenate([q_half[h // HEADS_PER_TILE, h % HEADS_PER_TILE][rows] for h in heads], axis=0)
            s = lax.dot_general(qs, k_var[g][krows], (((1,), (1,)), ((), ())), preferred_element_type=F32)
            logits = s + bias_ref[variant, g]
            m = jnp.maximum(jnp.max(logits, axis=-1, keepdims=True), sinks[g])
            p = jnp.exp(logits - m)
            denom = jnp.sum(p, axis=-1, keepdims=True) + jnp.exp(sinks[g] - m)
            o = jnp.dot(p.astype(BF16), v_var[g][krows], preferred_element_type=F32) / denom
            for j, h in enumerate(heads):
                res[h] = o[j * BLOCK:(j + 1) * BLOCK]
        for tile in range(N_HEADS // HEADS_PER_TILE):
            o_ref[rows, tile * LANES:(tile + 1) * LANES] = jnp.where(
                low, res[HEADS_PER_TILE * tile], res[HEADS_PER_TILE * tile + 1])


def _attention(p, qg, kg, sink, *, batch, seq, blocks):
    n = batch * seq
    nb = seq // BLOCK
    tq = blocks * BLOCK
    ns = seq // tq
    assert HEADS_PER_TILE == 2 and blocks >= 2 and seq % tq == 0
    q_col = (3 * D_HYENA + 3 * D_SCONV) // D_ATTN
    k_col = (3 * D_HYENA + 3 * D_SCONV + D_ATTN) // D_KV
    v_col = k_col + 1
    assert q_col * D_ATTN == 3 * D_HYENA + 3 * D_SCONV and k_col * D_KV == q_col * D_ATTN + D_ATTN

    def kv_specs(col):
        prev = pl.BlockSpec((BLOCK, D_KV), lambda b, i: (b * nb + jnp.maximum(i * blocks - 1, 0), col))
        cur = pl.BlockSpec((tq, D_KV), lambda b, i: (b * ns + i, col))
        nxt = pl.BlockSpec((BLOCK, D_KV), lambda b, i: (b * nb + jnp.minimum((i + 1) * blocks, nb - 1), col))
        return [prev, cur, nxt]

    return pl.pallas_call(
        functools.partial(_attn_kernel, blocks=blocks),
        grid=(batch, ns),
        in_specs=[pl.BlockSpec((tq, D_ATTN), lambda b, i: (b * ns + i, q_col))] + kv_specs(k_col) + kv_specs(v_col) + [
            pl.BlockSpec((1, HEAD_DIM), lambda b, i: (0, 0)),
            pl.BlockSpec((1, HEAD_DIM), lambda b, i: (0, 0)),
            pl.BlockSpec((1, N_HEADS), lambda b, i: (0, 0)),
        ],
        out_specs=pl.BlockSpec((tq, D_ATTN), lambda b, i: (b * ns + i, 0)),
        out_shape=jax.ShapeDtypeStruct((n, D_ATTN), F32),
        scratch_shapes=[pltpu.VMEM((3, len(ATTN_GROUPS), ATTN_GROUP_ROWS, 3 * BLOCK), F32)],
        compiler_params=_params("arbitrary", "arbitrary"),
        name="attention",
    )(p, p, p, p, p, p, p, qg, kg, sink)


def _outproj_kernel(x_ref, yh_ref, ys_ref, ya_ref, g_ref, w_ref, unperm_ref, o_ref):
    d1 = D_HYENA
    d2 = D_HYENA + D_SCONV
    nh = _rms(yh_ref[...], g_ref[:, :d1]).astype(BF16)
    nh = jnp.dot(unperm_ref[...], nh, preferred_element_type=F32).astype(BF16)
    ns = _rms(ys_ref[...], g_ref[:, d1:d2]).astype(BF16)
    na = _rms(ya_ref[...], g_ref[:, d2:]).astype(BF16)
    acc = jnp.dot(nh, w_ref[:d1, :], preferred_element_type=F32)
    acc += jnp.dot(ns, w_ref[d1:d2, :], preferred_element_type=F32)
    acc += jnp.dot(na, w_ref[d2:, :], preferred_element_type=F32)
    o_ref[...] = x_ref[...] + acc


def _outproj(x, yh, ys, ya, g, w, unperm):
    n, d = x.shape
    tm = PERM_ROWS
    row = lambda i: (i, 0)
    return pl.pallas_call(
        _outproj_kernel,
        grid=(n // tm,),
        in_specs=[
            pl.BlockSpec((tm, d), row),
            pl.BlockSpec((tm, D_HYENA), row),
            pl.BlockSpec((tm, D_SCONV), row),
            pl.BlockSpec((tm, D_ATTN), row),
            pl.BlockSpec((1, w.shape[0]), lambda i: (0, 0)),
            pl.BlockSpec(w.shape, lambda i: (0, 0)),
            pl.BlockSpec(unperm.shape, lambda i: (0, 0)),
        ],
        out_specs=pl.BlockSpec((tm, d), row),
        out_shape=jax.ShapeDtypeStruct((n, d), F32),
        compiler_params=_params("parallel"),
        name="outproj",
    )(x, yh, ys, ya, g, w, unperm)


def _tile(n, want):
    t = min(n, want)
    assert n % t == 0
    return t


def _filter_features(seq):
    n = jnp.arange(seq, dtype=F32)
    t = n / float(max(seq - 1, 1))
    bands = jnp.linspace(1e-4, FILTER_BANDS - 1, FILTER_BANDS, dtype=F32)
    ang = (2.0 * math.pi / seq) * n[:, None] * bands[None, :]
    z = jnp.concatenate([t[:, None], jnp.cos(ang), jnp.sin(ang)], axis=-1)
    return jnp.pad(z, ((0, 0), (0, LANES - z.shape[1])))


def kernel(x, norm_ffn1, ffn1_w_gate, ffn1_w_up, ffn1_w_down, norm_mix, w_in, hyena_short_w, hyena_short_b,
           filt_w1, filt_b1, filt_w2, filt_b2, filt_w3, filt_b3, filt_freq, filt_w_out, hyena_bias, sconv_w,
           q_norm_g, k_norm_g, attn_sink, mix_out_g, w_out, norm_ffn2, ffn2_w_gate, ffn2_w_up, ffn2_w_down):
    batch, seq, d = x.shape
    depth = w_in.shape[0]
    n = batch * seq
    assert seq % PERM_ROWS == 0
    tm = _tile(n, 1024)
    tf = _tile(ffn1_w_gate.shape[2], 512)
    tn = _tile(w_in.shape[2], 1536)

    xf = x.reshape(n, d)
    row = lambda a: a.reshape(1, -1)
    tabs = _fft_tables(seq)
    z = _filter_features(seq)
    deltas = jnp.abs(jnp.linspace(HYENA_MIN_DECAY, HYENA_MAX_DECAY, D_HYENA, dtype=F32)).reshape(1, -1)

    ffn1_f32 = (ffn1_w_gate, ffn1_w_up, ffn1_w_down)
    ffn2_f32 = (ffn2_w_gate, ffn2_w_up, ffn2_w_down)
    ffn_w = tuple(_cast_bf16(w, 0) for w in ffn1_f32)
    proj_cast = tuple((w, l) for l in range(depth) for w in (w_in, w_out))

    for l in range(depth):
        cast = tuple((w, l) for w in ffn2_f32) + (proj_cast if l == 0 else ())
        xf, side = _ffn(xf, row(norm_ffn1[l]), *ffn_w, tm=tm, tf=tf, cast=cast)
        ffn_w = side[:3]
        if l == 0:
            proj_w = side[3:]

        p = _inproj(xf, row(norm_mix[l]), proj_w[2 * l], tm=tm, tn=tn)

        w1 = jnp.pad(filt_w1[l], ((0, z.shape[1] - filt_w1.shape[1]), (0, 0)))
        kpm, ny = _hyena_filter(z, w1, row(filt_b1[l]), filt_w2[l], row(filt_b2[l]), filt_w3[l], row(filt_b3[l]),
                                row(filt_freq[l]), filt_w_out[l], deltas, tabs["perm"])
        kf = _filter_fft(kpm, tabs)

        zp, x0p, y_sc = _mixprep(p, hyena_short_w[l], row(hyena_short_b[l]), sconv_w[l], tabs["perm"],
                                 batch=batch, seq=seq)
        y, yn = _hyena_fwd(zp, kf, ny, tabs, batch=batch, seq=seq)
        y_hy = _hyena_inv(y, yn, zp, x0p, row(hyena_bias[l]), tabs, batch=batch, seq=seq)

        y_at = _attention(p, row(q_norm_g[l]), row(k_norm_g[l]), row(attn_sink[l]), batch=batch, seq=seq,
                          blocks=_tile(seq // BLOCK, 8))

        xf = _outproj(xf, y_hy, y_sc, y_at, row(mix_out_g[l]), proj_w[2 * l + 1], tabs["unperm"])

        cast = tuple((w, l + 1) for w in ffn1_f32) if l + 1 < depth else ()
        xf, ffn_w = _ffn(xf, row(norm_ffn2[l]), *ffn_w, tm=tm, tf=tf, cast=cast)

    return xf.reshape(batch, seq, d)
```

```python
import functools
import math

import jax
import jax.numpy as jnp
from jax import lax
from jax.experimental import pallas as pl
from jax.experimental.pallas import tpu as pltpu

F32 = jnp.float32
BF16 = jnp.bfloat16

EPS = 1e-6
NEG_INF = -1e30
LOG2_E = 1.0 / math.log(2.0)

D_HYENA = 512
D_SCONV = 512
N_HEADS = 16
N_KV_HEADS = 4
GROUP = N_HEADS // N_KV_HEADS
HEAD_DIM = 64
D_ATTN = N_HEADS * HEAD_DIM
D_KV = N_KV_HEADS * HEAD_DIM
WINDOW = 128
BLOCK = 128
FILTER_BANDS = 16
FILTER_HIDDEN = 64
HYENA_MIN_DECAY = math.log(1e-2) / 1.5
HYENA_MAX_DECAY = math.log(1e-2) / 0.3
HYENA_SHIFT = 0.05

V7X_VMEM_LIMIT_BYTES = 60 * 1024 * 1024
HALO_ROWS = 16
LANES = 128
F32_TILE_ROWS = 8

FFT_B = 32
FFT_D = FFT_B // 2
PERM_ROWS = FFT_B * HALO_ROWS
FFT_LANES = 256


def _params(*sem):
    return pltpu.CompilerParams(dimension_semantics=sem, vmem_limit_bytes=V7X_VMEM_LIMIT_BYTES)


def _rms(x, g):
    ms = jnp.mean(x * x, axis=-1, keepdims=True)
    return x * lax.rsqrt(ms + EPS) * g


CAST_BLOCK_BYTES = 6 * 1024 * 1024


def _cast_kernel(w_ref, o_ref):
    o_ref[...] = w_ref[...].astype(BF16)


def _cast_bf16(w, layer):
    _, r, c = w.shape
    rows = r
    while rows * c * 4 > CAST_BLOCK_BYTES and rows % 32 == 0:
        rows //= 2
    return pl.pallas_call(
        _cast_kernel,
        grid=(r // rows,),
        in_specs=[pl.BlockSpec((None, rows, c), lambda i: (layer, i, 0))],
        out_specs=pl.BlockSpec((rows, c), lambda i: (i, 0)),
        out_shape=jax.ShapeDtypeStruct((r, c), BF16),
        compiler_params=_params("parallel"),
        name="cast_bf16",
    )(w)


def _ffn_kernel(x_ref, g_ref, wg_ref, wu_ref, wd_ref, *rest, n_side):
    side_in, o_ref, side_out, h_ref = rest[:n_side], rest[n_side], rest[n_side + 1:2 * n_side + 1], rest[-1]

    @pl.when(pl.program_id(1) == 0)
    def _():
        x = x_ref[...]
        h_ref[...] = _rms(x, g_ref[...]).astype(BF16)
        o_ref[...] = x

    h = h_ref[...]
    gate = jnp.dot(h, wg_ref[...], preferred_element_type=F32)
    up = jnp.dot(h, wu_ref[...], preferred_element_type=F32)
    a = (gate * jax.nn.sigmoid(gate) * up * 0.5).astype(BF16)
    o_ref[...] += jnp.dot(a, wd_ref[...], preferred_element_type=F32)
    for src, dst in zip(side_in, side_out):
        dst[...] = src[...].astype(BF16)


def _cast_blocks(shape, ni, nj):
    r, c = shape
    br = r // ni
    assert r % ni == 0 and br % HALO_ROWS == 0
    ncb = nj
    while c % ncb or (c // ncb) % LANES:
        ncb -= 1
    return (br, c // ncb), (lambda i, j: (i, jnp.minimum(j, ncb - 1)))


def _ffn(x, g, wg, wu, wd, *, tm, tf, cast=()):
    n, d = x.shape
    ff = wg.shape[1]
    ni, nj = n // tm, ff // tf
    side_args, side_in_specs, side_out_specs, side_shapes = [], [], [], []
    for w, layer in cast:
        blk, idx = _cast_blocks(w.shape[1:], ni, nj)
        side_args.append(w)
        side_in_specs.append(pl.BlockSpec((None,) + blk, lambda i, j, idx=idx, layer=layer: (layer,) + idx(i, j)))
        side_out_specs.append(pl.BlockSpec(blk, idx))
        side_shapes.append(jax.ShapeDtypeStruct(w.shape[1:], BF16))
    out = pl.pallas_call(
        functools.partial(_ffn_kernel, n_side=len(side_args)),
        grid=(ni, nj),
        in_specs=[
            pl.BlockSpec((tm, d), lambda i, j: (i, 0)),
            pl.BlockSpec((1, d), lambda i, j: (0, 0)),
            pl.BlockSpec((d, tf), lambda i, j: (0, j)),
            pl.BlockSpec((d, tf), lambda i, j: (0, j)),
            pl.BlockSpec((tf, d), lambda i, j: (j, 0)),
        ] + side_in_specs,
        out_specs=[pl.BlockSpec((tm, d), lambda i, j: (i, 0))] + side_out_specs,
        out_shape=[jax.ShapeDtypeStruct((n, d), F32)] + side_shapes,
        scratch_shapes=[pltpu.VMEM((tm, d), BF16)],
        compiler_params=_params("parallel", "arbitrary"),
        name="ffn",
    )(x, g, wg, wu, wd, *side_args)
    return out[0], tuple(out[1:])


def _inproj_kernel(x_ref, g_ref, w_ref, o_ref, h_ref):
    @pl.when(pl.program_id(1) == 0)
    def _():
        h_ref[...] = _rms(x_ref[...], g_ref[...]).astype(BF16)

    o_ref[...] = jnp.dot(h_ref[...], w_ref[...], preferred_element_type=F32).astype(o_ref.dtype)


def _inproj(x, g, w, *, tm, tn):
    n, d = x.shape
    dp = w.shape[1]
    return pl.pallas_call(
        _inproj_kernel,
        grid=(n // tm, dp // tn),
        in_specs=[
            pl.BlockSpec((tm, d), lambda i, j: (i, 0)),
            pl.BlockSpec((1, d), lambda i, j: (0, 0)),
            pl.BlockSpec((d, tn), lambda i, j: (0, j)),
        ],
        out_specs=pl.BlockSpec((tm, tn), lambda i, j: (i, j)),
        out_shape=jax.ShapeDtypeStruct((n, dp), BF16),
        scratch_shapes=[pltpu.VMEM((tm, d), BF16)],
        compiler_params=_params("parallel", "arbitrary"),
        name="inproj",
    )(x, g, w)


def _fft_tables(seq):
    period = 2 * seq
    t1 = period // FFT_B
    na = seq // FFT_B
    b = jnp.arange(FFT_B, dtype=jnp.int32)[:, None, None]
    c = jnp.arange(t1, dtype=jnp.int32)[None, :, None]
    a = jnp.arange(na, dtype=jnp.int32)[None, None, :]
    ang_a = ((c * a) % t1).astype(F32) * (2.0 * math.pi / t1)
    ang_b = ((c * b) % period).astype(F32) * (2.0 * math.pi / period)
    cos_a, sin_a, cos_b, sin_b = jnp.cos(ang_a), jnp.sin(ang_a), jnp.cos(ang_b), jnp.sin(ang_b)
    f1 = jnp.concatenate([cos_a * cos_b - sin_a * sin_b, -(sin_a * cos_b + cos_a * sin_b)], axis=1)
    d = jnp.arange(FFT_D, dtype=jnp.int32)[:, None]
    bb = jnp.arange(FFT_B, dtype=jnp.int32)[None, :]
    ang2 = ((d * bb) % FFT_B).astype(F32) * (2.0 * math.pi / FFT_B)
    eye = jnp.eye(F32_TILE_ROWS, dtype=F32)
    cr = jnp.kron(jnp.cos(ang2), eye)
    sr = jnp.kron(jnp.sin(ang2), eye)
    mb = jnp.block([[cr, sr], [-sr, cr]])
    mbi = jnp.block([[cr.T, -sr.T], [sr.T, cr.T]])
    r = jnp.arange(PERM_ROWS, dtype=jnp.int32)
    src = (r % HALO_ROWS) * FFT_B + r // HALO_ROWS
    perm = (src[:, None] == r[None, :]).astype(BF16)
    return dict(f1=f1.astype(BF16), f1t=jnp.swapaxes(f1, 1, 2).astype(BF16), mb=mb.astype(BF16),
                mbi=mbi.astype(BF16), perm=perm, unperm=perm.T)


def _filter_kernel(z_ref, w1_ref, b1_ref, w2_ref, b2_ref, w3_ref, b3_ref, fr_ref, wo_ref, dl_ref, perm_ref,
                   kpm_ref, ny_ref, *, seq, rows):
    i = pl.program_id(0)
    hp = lax.Precision.HIGHEST
    fr = fr_ref[...]
    h = jnp.sin(fr * (jnp.dot(z_ref[...], w1_ref[...], precision=hp, preferred_element_type=F32) + b1_ref[...]))
    h = jnp.sin(fr * (jnp.dot(h, w2_ref[...], precision=hp, preferred_element_type=F32) + b2_ref[...]))
    h = jnp.sin(fr * (jnp.dot(h, w3_ref[...], precision=hp, preferred_element_type=F32) + b3_ref[...]))
    h = jnp.dot(h, wo_ref[...], precision=hp, preferred_element_type=F32)
    n = lax.broadcasted_iota(jnp.int32, (rows, 1), 0) + i * rows
    t = n.astype(F32) / float(max(seq - 1, 1))
    window = jnp.exp(-t * dl_ref[...]) + HYENA_SHIFT
    hf = h[:, :D_HYENA] * window
    hb = jnp.where(n == 0, 0.0, h[:, D_HYENA:] * window)
    kp = hf + hb
    perm = perm_ref[...]
    kpm_ref[0] = jnp.dot(perm, kp.astype(BF16), preferred_element_type=F32).astype(BF16)
    kpm_ref[1] = jnp.dot(perm, (hf - hb).astype(BF16), preferred_element_type=F32).astype(BF16)
    sign = jnp.where((n & 1) == 0, 1.0, -1.0)

    @pl.when(i == 0)
    def _():
        ny_ref[...] = jnp.zeros_like(ny_ref)

    ny_ref[...] += jnp.sum(kp * sign, axis=0, keepdims=True)


def _hyena_filter(z, w1, b1, w2, b2, w3, b3, fr, wo, deltas, perm):
    seq, zw = z.shape
    rows = PERM_ROWS
    full = lambda a: pl.BlockSpec(a.shape, lambda i: (0,) * a.ndim)
    return pl.pallas_call(
        functools.partial(_filter_kernel, seq=seq, rows=rows),
        grid=(seq // rows,),
        in_specs=[pl.BlockSpec((rows, zw), lambda i: (i, 0))] + [
            full(a) for a in (w1, b1, w2, b2, w3, b3, fr, wo, deltas, perm)],
        out_specs=[
            pl.BlockSpec((2, rows, D_HYENA), lambda i: (0, i, 0)),
            pl.BlockSpec((1, D_HYENA), lambda i: (0, 0)),
        ],
        out_shape=[
            jax.ShapeDtypeStruct((2, seq, D_HYENA), BF16),
            jax.ShapeDtypeStruct((1, D_HYENA), F32),
        ],
        compiler_params=_params("arbitrary"),
        name="hyena_filter",
    )(z, w1, b1, w2, b2, w3, b3, fr, wo, deltas, perm)


def _shift_rows(u, halo_prev, halo_next, first, last):
    rows = u.shape[0]
    r = lax.broadcasted_iota(jnp.int32, (rows, 1), 0)
    prev_row = jnp.where(first, 0.0, halo_prev)
    next_row = jnp.where(last, 0.0, halo_next)
    um = jnp.where(r == 0, prev_row, pltpu.roll(u, 1, axis=0))
    up = jnp.where(r == rows - 1, next_row, pltpu.roll(u, rows - 1, axis=0))
    return um, up


def _mixprep_kernel(p_ref, pp_ref, pn_ref, hw_ref, hb_ref, sw_ref, perm_ref, zp_ref, x0p_ref, ysc_ref):
    first = pl.program_id(1) == 0
    last = pl.program_id(1) == pl.num_programs(1) - 1
    nh = 3 * D_HYENA
    hr = HALO_ROWS
    p = p_ref[:, :nh].astype(F32)
    pm, pp = _shift_rows(p, pp_ref[hr - 1:hr, :nh].astype(F32), pn_ref[0:1, :nh].astype(F32), first, last)
    u = pm * hw_ref[0:1, :] + p * hw_ref[1:2, :] + pp * hw_ref[2:3, :] + hb_ref[...]
    zv = u[:, 2 * D_HYENA:] * u[:, D_HYENA:2 * D_HYENA]
    perm = perm_ref[...]
    zp_ref[...] = jnp.dot(perm, zv.astype(BF16), preferred_element_type=F32).astype(BF16)
    x0p_ref[...] = jnp.dot(perm, u[:, :D_HYENA].astype(BF16), preferred_element_type=F32).astype(BF16)
    ds = D_SCONV
    gb = p_ref[:, nh:nh + ds].astype(F32)
    ch = p_ref[:, nh + ds:nh + 2 * ds].astype(F32) * p_ref[:, nh + 2 * ds:nh + 3 * ds].astype(F32)
    ch_prev = pp_ref[hr - 1:hr, nh + ds:nh + 2 * ds].astype(F32) * pp_ref[hr - 1:hr, nh + 2 * ds:nh + 3 * ds].astype(F32)
    ch_next = pn_ref[0:1, nh + ds:nh + 2 * ds].astype(F32) * pn_ref[0:1, nh + 2 * ds:nh + 3 * ds].astype(F32)
    cm, cp = _shift_rows(ch, ch_prev, ch_next, first, last)
    ysc_ref[...] = gb * (cm * sw_ref[0:1, :] + ch * sw_ref[1:2, :] + cp * sw_ref[2:3, :])


def _mixprep(p, hw, hb, sw, perm, *, batch, seq):
    n = batch * seq
    tl = PERM_ROWS
    nl = seq // tl
    width = 3 * D_HYENA + 3 * D_SCONV
    hb8 = tl // HALO_ROWS
    last8 = seq // HALO_ROWS - 1
    row = lambda b, l: (b * nl + l, 0)
    const = lambda a: pl.BlockSpec(a.shape, lambda b, l: (0, 0))
    return pl.pallas_call(
        _mixprep_kernel,
        grid=(batch, nl),
        in_specs=[
            pl.BlockSpec((tl, width), row),
            pl.BlockSpec((HALO_ROWS, width), lambda b, l: (b * (last8 + 1) + jnp.maximum(l * hb8 - 1, 0), 0)),
            pl.BlockSpec((HALO_ROWS, width), lambda b, l: (b * (last8 + 1) + jnp.minimum((l + 1) * hb8, last8), 0)),
            const(hw), const(hb), const(sw), const(perm),
        ],
        out_specs=[pl.BlockSpec((tl, D_HYENA), row)] * 3,
        out_shape=[
            jax.ShapeDtypeStruct((n, D_HYENA), BF16),
            jax.ShapeDtypeStruct((n, D_HYENA), BF16),
            jax.ShapeDtypeStruct((n, D_SCONV), F32),
        ],
        compiler_params=_params("parallel", "parallel"),
        name="mixprep",
    )(p, p, p, hw, hb, sw, perm)


def _fft_forward(zp_ref, f1_ref, mb_ref, a_scr, *, seq, emit):
    t1 = 2 * seq // FFT_B
    r8 = F32_TILE_ROWS
    for b in range(FFT_B):
        xb = jnp.concatenate([zp_ref[blk * PERM_ROWS + b * HALO_ROWS:blk * PERM_ROWS + (b + 1) * HALO_ROWS, :]
                              for blk in range(seq // PERM_ROWS)], axis=0)
        a_scr[b] = jnp.dot(f1_ref[b], xb, preferred_element_type=F32)
    mb = mb_ref[...]
    half = FFT_D * r8
    for ch in range(t1 // r8):
        re = [a_scr[b, ch * r8:(ch + 1) * r8, :] for b in range(FFT_B)]
        im = [a_scr[b, t1 + ch * r8:t1 + (ch + 1) * r8, :] for b in range(FFT_B)]
        xs = jnp.dot(mb, jnp.concatenate(re + im, axis=0).astype(BF16), preferred_element_type=F32)
        emit(ch, xs[:half], xs[half:])


def _filter_fft_kernel(kpm_ref, f1_ref, mb_ref, k_ref, a_scr, *, seq):
    part = pl.program_id(0)
    half = FFT_D * F32_TILE_ROWS

    def emit(ch, re, im):
        val = jnp.where(part == 0, re, im) * (1.0 / seq)
        if ch == 0:
            val = jnp.where(lax.broadcasted_iota(jnp.int32, (half, 1), 0) == 0, 0.5 * val, val)
        k_ref[ch * half:(ch + 1) * half, :] = val.astype(BF16)

    _fft_forward(kpm_ref, f1_ref, mb_ref, a_scr, seq=seq, emit=emit)


def _filter_fft(kpm, tabs):
    _, seq, width = kpm.shape
    w = FFT_LANES
    f1, mb = tabs["f1"], tabs["mb"]
    return pl.pallas_call(
        functools.partial(_filter_fft_kernel, seq=seq),
        grid=(2, width // w),
        in_specs=[
            pl.BlockSpec((None, seq, w), lambda p, h: (p, 0, h)),
            pl.BlockSpec(f1.shape, lambda p, h: (0, 0, 0)),
            pl.BlockSpec(mb.shape, lambda p, h: (0, 0)),
        ],
        out_specs=pl.BlockSpec((None, seq, w), lambda p, h: (p, 0, h)),
        out_shape=jax.ShapeDtypeStruct((2, seq, width), BF16),
        scratch_shapes=[pltpu.VMEM((FFT_B, f1.shape[1], w), F32)],
        compiler_params=_params("parallel", "parallel"),
        name="filter_fft",
    )(kpm, f1, mb)


def _hyena_fwd_kernel(zp_ref, f1_ref, mb_ref, k_ref, ny_ref, y_ref, yn_ref, a_scr, *, seq):
    half = FFT_D * F32_TILE_ROWS
    r = lax.broadcasted_iota(jnp.int32, (seq, 1), 0)
    sign = jnp.where(((r // HALO_ROWS) & 1) == 0, 1.0, -1.0)
    xn = jnp.sum(zp_ref[...].astype(F32) * sign, axis=0, keepdims=True)
    yn_ref[...] = (xn * ny_ref[...] * (0.5 / seq)).reshape(yn_ref.shape)

    def emit(ch, re, im):
        kr = k_ref[0, ch * half:(ch + 1) * half, :].astype(F32)
        ki = k_ref[1, ch * half:(ch + 1) * half, :].astype(F32)
        y_ref[2 * ch * half:(2 * ch + 1) * half, :] = (re * kr - im * ki).astype(BF16)
        y_ref[(2 * ch + 1) * half:(2 * ch + 2) * half, :] = (re * ki + im * kr).astype(BF16)

    _fft_forward(zp_ref, f1_ref, mb_ref, a_scr, seq=seq, emit=emit)


def _hyena_fwd(zp, k, ny, tabs, *, batch, seq):
    width = zp.shape[1]
    w = FFT_LANES
    f1, mb = tabs["f1"], tabs["mb"]
    return pl.pallas_call(
        functools.partial(_hyena_fwd_kernel, seq=seq),
        grid=(batch, width // w),
        in_specs=[
            pl.BlockSpec((seq, w), lambda b, h: (b, h)),
            pl.BlockSpec(f1.shape, lambda b, h: (0, 0, 0)),
            pl.BlockSpec(mb.shape, lambda b, h: (0, 0)),
            pl.BlockSpec((2, seq, w), lambda b, h: (0, 0, h)),
            pl.BlockSpec((1, w), lambda b, h: (0, h)),
        ],
        out_specs=[
            pl.BlockSpec((2 * seq, w), lambda b, h: (b, h)),
            pl.BlockSpec((1, 1, w), lambda b, h: (b, 0, h)),
        ],
        out_shape=[
            jax.ShapeDtypeStruct((batch * 2 * seq, width), BF16),
            jax.ShapeDtypeStruct((batch, 1, width), F32),
        ],
        scratch_shapes=[pltpu.VMEM((FFT_B, f1.shape[1], w), F32)],
        compiler_params=_params("parallel", "parallel"),
        name="hyena_fwd",
    )(zp, f1, mb, k, ny)


def _hyena_inv_kernel(y_ref, mbi_ref, f1t_ref, yn_ref, zp_ref, x0p_ref, bd_ref, o_ref, g_scr, *, seq):
    t1 = 2 * seq // FFT_B
    r8 = F32_TILE_ROWS
    nch = t1 // r8
    blk_rows = 2 * FFT_D * r8
    mbi = mbi_ref[...]
    for ch in range(nch):
        g_scr[ch] = jnp.dot(mbi, y_ref[ch * blk_rows:(ch + 1) * blk_rows, :], preferred_element_type=F32)
    half = FFT_B * r8
    yn = yn_ref[0]
    bd = bd_ref[...]
    for b in range(FFT_B):
        re = [g_scr[ch, b * r8:(b + 1) * r8, :] for ch in range(nch)]
        im = [g_scr[ch, half + b * r8:half + (b + 1) * r8, :] for ch in range(nch)]
        yb = jnp.dot(f1t_ref[b], jnp.concatenate(re + im, axis=0).astype(BF16), preferred_element_type=F32)
        nyq = yn if b % 2 == 0 else -yn
        for blk in range(seq // PERM_ROWS):
            rows = slice(blk * PERM_ROWS + b * HALO_ROWS, blk * PERM_ROWS + (b + 1) * HALO_ROWS)
            conv = yb[blk * HALO_ROWS:(blk + 1) * HALO_ROWS, :]
            o_ref[rows, :] = x0p_ref[rows, :].astype(F32) * (conv + nyq + zp_ref[rows, :].astype(F32) * bd)


def _hyena_inv(y, yn, zp, x0p, bd, tabs, *, batch, seq):
    width = y.shape[1]
    w = FFT_LANES
    f1t, mbi = tabs["f1t"], tabs["mbi"]
    blk = pl.BlockSpec((seq, w), lambda b, h: (b, h))
    return pl.pallas_call(
        functools.partial(_hyena_inv_kernel, seq=seq),
        grid=(batch, width // w),
        in_specs=[
            pl.BlockSpec((2 * seq, w), lambda b, h: (b, h)),
            pl.BlockSpec(mbi.shape, lambda b, h: (0, 0)),
            pl.BlockSpec(f1t.shape, lambda b, h: (0, 0, 0)),
            pl.BlockSpec((1, 1, w), lambda b, h: (b, 0, h)),
            blk, blk,
            pl.BlockSpec((1, w), lambda b, h: (0, h)),
        ],
        out_specs=blk,
        out_shape=jax.ShapeDtypeStruct((batch * seq, width), F32),
        scratch_shapes=[pltpu.VMEM((2 * seq // FFT_B // F32_TILE_ROWS, 2 * FFT_B * F32_TILE_ROWS, w), F32)],
        compiler_params=_params("parallel", "parallel"),
        name="hyena_inv",
    )(y, mbi, f1t, yn, zp, x0p, bd)


HEADS_PER_TILE = LANES // HEAD_DIM
ATTN_GROUPS = tuple(
    tuple(h for h in range(N_HEADS)
          if (h // GROUP) // HEADS_PER_TILE == tile and (h % HEADS_PER_TILE != (h // GROUP) % HEADS_PER_TILE) == swapped)
    for tile in range(N_KV_HEADS // HEADS_PER_TILE) for swapped in (False, True))
ATTN_GROUP_ROWS = len(ATTN_GROUPS[0]) * BLOCK


def _attn_kernel(q_ref, kp_ref, kc_ref, kn_ref, vp_ref, vc_ref, vn_ref, qg_ref, kg_ref, sink_ref, o_ref, bias_ref,
                 *, blocks):
    step = pl.program_id(1)
    last_step = pl.num_programs(1) - 1
    keys = 3 * BLOCK
    grows = ATTN_GROUP_ROWS

    @pl.when((pl.program_id(0) == 0) & (step == 0))
    def _():
        kj = lax.broadcasted_iota(jnp.int32, (keys, BLOCK), 0)
        qi = lax.broadcasted_iota(jnp.int32, (keys, BLOCK), 1)
        dist = jnp.abs(qi - (kj - BLOCK))
        distf = dist.astype(F32)
        for variant, (lo, hi) in enumerate(((BLOCK, keys), (0, keys), (0, 2 * BLOCK))):
            ok = (dist <= WINDOW) & (kj >= lo) & (kj < hi)
            for g, heads in enumerate(ATTN_GROUPS):
                for j, h in enumerate(heads):
                    slope = LOG2_E * 2.0 ** (-8.0 * (h + 1) / N_HEADS)
                    bias_ref[variant, g, :, j * BLOCK:(j + 1) * BLOCK] = jnp.where(ok, -slope * distf, NEG_INF)

    seg_r = lax.broadcasted_iota(jnp.int32, (D_KV, D_KV), 0) // HEAD_DIM
    seg_c = lax.broadcasted_iota(jnp.int32, (D_KV, D_KV), 1) // HEAD_DIM
    seg_mean = jnp.where(seg_r == seg_c, 1.0 / HEAD_DIM, 0.0).astype(BF16)

    def head_norm(x, g):
        ms = jnp.dot((x * x).astype(BF16), seg_mean, preferred_element_type=F32)
        return x * lax.rsqrt(ms + EPS) * g

    low = lax.broadcasted_iota(jnp.int32, (1, LANES), 1) < HEAD_DIM
    qg = jnp.concatenate([qg_ref[...]] * GROUP, axis=1) * (LOG2_E * HEAD_DIM ** -0.5)
    kg = jnp.concatenate([kg_ref[...]] * N_KV_HEADS, axis=1)
    k = jnp.concatenate([kp_ref[...], kc_ref[...], kn_ref[...]], axis=0).astype(F32)
    k = head_norm(k, kg)
    v = jnp.concatenate([vp_ref[...], vc_ref[...], vn_ref[...]], axis=0).astype(F32)
    k_var, v_var = [], []
    for t in range(D_KV // LANES):
        kt = k[:, t * LANES:(t + 1) * LANES]
        vt = v[:, t * LANES:(t + 1) * LANES]
        k_var += [kt.astype(BF16), pltpu.roll(kt, HEAD_DIM, axis=1).astype(BF16)]
        v_var += [vt.astype(BF16), pltpu.roll(vt, HEAD_DIM, axis=1).astype(BF16)]

    q_half = {}
    for c in range(N_KV_HEADS):
        qn = head_norm(q_ref[:, c * D_KV:(c + 1) * D_KV].astype(F32), qg)
        for u in range(D_KV // LANES):
            qt = qn[:, u * LANES:(u + 1) * LANES]
            tile = c * (D_KV // LANES) + u
            q_half[tile, 0] = jnp.where(low, qt, 0.0).astype(BF16)
            q_half[tile, 1] = jnp.where(low, 0.0, qt).astype(BF16)

    lane_head = lax.broadcasted_iota(jnp.int32, (1, grows), 1) // BLOCK
    sinks = []
    for heads in ATTN_GROUPS:
        srow = jnp.zeros((1, grows), F32)
        for j, h in enumerate(heads):
            srow = jnp.where(lane_head == j, sink_ref[0:1, h:h + 1], srow)
        sinks.append(srow * LOG2_E)

    for qb in range(blocks):
        rows = slice(qb * BLOCK, (qb + 1) * BLOCK)
        krows = slice(qb * BLOCK, qb * BLOCK + keys)
        if qb == 0:
            variant = jnp.where(step == 0, 0, 1)
        elif qb == blocks - 1:
            variant = jnp.where(step == last_step, 2, 1)
        else:
            variant = 1
        res = {}
        for g, heads in enumerate(ATTN_GROUPS):
            qs = jnp.concatenate([q_half[h // HEADS_PER_TILE, h % HEADS_PER_TILE][rows] for h in heads], axis=0)
            st = lax.dot_general(k_var[g][krows], qs, (((1,), (1,)), ((), ())), preferred_element_type=F32)
            logits = st + bias_ref[variant, g]
            m = jnp.maximum(jnp.max(logits, axis=0, keepdims=True), sinks[g])
            p = jnp.exp2(logits - m)
            denom = jnp.sum(p, axis=0, keepdims=True) + jnp.exp2(sinks[g] - m)
            ot = lax.dot_general(v_var[g][krows], p.astype(BF16), (((0,), (0,)), ((), ())),
                                 preferred_element_type=F32) / denom
            for j, h in enumerate(heads):
                res[h] = ot[:, j * BLOCK:(j + 1) * BLOCK].T
        for tile in range(N_HEADS // HEADS_PER_TILE):
            o_ref[rows, tile * LANES:(tile + 1) * LANES] = jnp.where(
                low, res[HEADS_PER_TILE * tile], res[HEADS_PER_TILE * tile + 1])


def _attention(p, qg, kg, sink, *, batch, seq, blocks):
    n = batch * seq
    nb = seq // BLOCK
    tq = blocks * BLOCK
    ns = seq // tq
    assert HEADS_PER_TILE == 2 and blocks >= 2 and seq % tq == 0
    q_col = (3 * D_HYENA + 3 * D_SCONV) // D_ATTN
    k_col = (3 * D_HYENA + 3 * D_SCONV + D_ATTN) // D_KV
    v_col = k_col + 1
    assert q_col * D_ATTN == 3 * D_HYENA + 3 * D_SCONV and k_col * D_KV == q_col * D_ATTN + D_ATTN

    def kv_specs(col):
        prev = pl.BlockSpec((BLOCK, D_KV), lambda b, i: (b * nb + jnp.maximum(i * blocks - 1, 0), col))
        cur = pl.BlockSpec((tq, D_KV), lambda b, i: (b * ns + i, col))
        nxt = pl.BlockSpec((BLOCK, D_KV), lambda b, i: (b * nb + jnp.minimum((i + 1) * blocks, nb - 1), col))
        return [prev, cur, nxt]

    return pl.pallas_call(
        functools.partial(_attn_kernel, blocks=blocks),
        grid=(batch, ns),
        in_specs=[pl.BlockSpec((tq, D_ATTN), lambda b, i: (b * ns + i, q_col))] + kv_specs(k_col) + kv_specs(v_col) + [
            pl.BlockSpec((1, HEAD_DIM), lambda b, i: (0, 0)),
            pl.BlockSpec((1, HEAD_DIM), lambda b, i: (0, 0)),
            pl.BlockSpec((1, N_HEADS), lambda b, i: (0, 0)),
        ],
        out_specs=pl.BlockSpec((tq, D_ATTN), lambda b, i: (b * ns + i, 0)),
        out_shape=jax.ShapeDtypeStruct((n, D_ATTN), F32),
        scratch_shapes=[pltpu.VMEM((3, len(ATTN_GROUPS), 3 * BLOCK, ATTN_GROUP_ROWS), F32)],
        compiler_params=_params("arbitrary", "arbitrary"),
        name="attention",
    )(p, p, p, p, p, p, p, qg, kg, sink)


def _outproj_kernel(x_ref, yh_ref, ys_ref, ya_ref, g_ref, w_ref, unperm_ref, o_ref):
    d1 = D_HYENA
    d2 = D_HYENA + D_SCONV
    nh = _rms(yh_ref[...], g_ref[:, :d1]).astype(BF16)
    nh = jnp.dot(unperm_ref[...], nh, preferred_element_type=F32).astype(BF16)
    ns = _rms(ys_ref[...], g_ref[:, d1:d2]).astype(BF16)
    na = _rms(ya_ref[...], g_ref[:, d2:]).astype(BF16)
    acc = jnp.dot(nh, w_ref[:d1, :], preferred_element_type=F32)
    acc += jnp.dot(ns, w_ref[d1:d2, :], preferred_element_type=F32)
    acc += jnp.dot(na, w_ref[d2:, :], preferred_element_type=F32)
    o_ref[...] = x_ref[...] + acc


def _outproj(x, yh, ys, ya, g, w, unperm):
    n, d = x.shape
    tm = PERM_ROWS
    row = lambda i: (i, 0)
    return pl.pallas_call(
        _outproj_kernel,
        grid=(n // tm,),
        in_specs=[
            pl.BlockSpec((tm, d), row),
            pl.BlockSpec((tm, D_HYENA), row),
            pl.BlockSpec((tm, D_SCONV), row),
            pl.BlockSpec((tm, D_ATTN), row),
            pl.BlockSpec((1, w.shape[0]), lambda i: (0, 0)),
            pl.BlockSpec(w.shape, lambda i: (0, 0)),
            pl.BlockSpec(unperm.shape, lambda i: (0, 0)),
        ],
        out_specs=pl.BlockSpec((tm, d), row),
        out_shape=jax.ShapeDtypeStruct((n, d), F32),
        compiler_params=_params("parallel"),
        name="outproj",
    )(x, yh, ys, ya, g, w, unperm)


def _tile(n, want):
    t = min(n, want)
    assert n % t == 0
    return t


def _filter_features(seq):
    n = jnp.arange(seq, dtype=F32)
    t = n / float(max(seq - 1, 1))
    bands = jnp.linspace(1e-4, FILTER_BANDS - 1, FILTER_BANDS, dtype=F32)
    ang = (2.0 * math.pi / seq) * n[:, None] * bands[None, :]
    z = jnp.concatenate([t[:, None], jnp.cos(ang), jnp.sin(ang)], axis=-1)
    return jnp.pad(z, ((0, 0), (0, LANES - z.shape[1])))


def kernel(x, norm_ffn1, ffn1_w_gate, ffn1_w_up, ffn1_w_down, norm_mix, w_in, hyena_short_w, hyena_short_b,
           filt_w1, filt_b1, filt_w2, filt_b2, filt_w3, filt_b3, filt_freq, filt_w_out, hyena_bias, sconv_w,
           q_norm_g, k_norm_g, attn_sink, mix_out_g, w_out, norm_ffn2, ffn2_w_gate, ffn2_w_up, ffn2_w_down):
    batch, seq, d = x.shape
    depth = w_in.shape[0]
    n = batch * seq
    assert seq % PERM_ROWS == 0
    tm = _tile(n, 1024)
    tf = _tile(ffn1_w_gate.shape[2], 512)
    tn = _tile(w_in.shape[2], 1536)

    xf = x.reshape(n, d)
    row = lambda a: a.reshape(1, -1)
    tabs = _fft_tables(seq)
    z = _filter_features(seq)
    deltas = jnp.abs(jnp.linspace(HYENA_MIN_DECAY, HYENA_MAX_DECAY, D_HYENA, dtype=F32)).reshape(1, -1)

    ffn1_f32 = (ffn1_w_gate, ffn1_w_up, ffn1_w_down)
    ffn2_f32 = (ffn2_w_gate, ffn2_w_up, ffn2_w_down)
    ffn_w = tuple(_cast_bf16(w, 0) for w in ffn1_f32)
    proj_cast = tuple((w, l) for l in range(depth) for w in (w_in, w_out))

    for l in range(depth):
        cast = tuple((w, l) for w in ffn2_f32) + (proj_cast if l == 0 else ())
        xf, side = _ffn(xf, row(norm_ffn1[l]), *ffn_w, tm=tm, tf=tf, cast=cast)
        ffn_w = side[:3]
        if l == 0:
            proj_w = side[3:]

        p = _inproj(xf, row(norm_mix[l]), proj_w[2 * l], tm=tm, tn=tn)

        w1 = jnp.pad(filt_w1[l], ((0, z.shape[1] - filt_w1.shape[1]), (0, 0)))
        kpm, ny = _hyena_filter(z, w1, row(filt_b1[l]), filt_w2[l], row(filt_b2[l]), filt_w3[l], row(filt_b3[l]),
                                row(filt_freq[l]), filt_w_out[l], deltas, tabs["perm"])
        kf = _filter_fft(kpm, tabs)

        zp, x0p, y_sc = _mixprep(p, hyena_short_w[l], row(hyena_short_b[l]), sconv_w[l], tabs["perm"],
                                 batch=batch, seq=seq)
        y, yn = _hyena_fwd(zp, kf, ny, tabs, batch=batch, seq=seq)
        y_hy = _hyena_inv(y, yn, zp, x0p, row(hyena_bias[l]), tabs, batch=batch, seq=seq)

        y_at = _attention(p, row(q_norm_g[l]), row(k_norm_g[l]), row(attn_sink[l]), batch=batch, seq=seq,
                          blocks=_tile(seq // BLOCK, 8))

        xf = _outproj(xf, y_hy, y_sc, y_at, row(mix_out_g[l]), proj_w[2 * l + 1], tabs["unperm"])

        cast = tuple((w, l + 1) for w in ffn1_f32) if l + 1 < depth else ()
        xf, ffn_w = _ffn(xf, row(norm_ffn2[l]), *ffn_w, tm=tm, tf=tf, cast=cast)

    return xf.reshape(batch, seq, d)
```

```python
import functools
import math

import jax
import jax.numpy as jnp
from jax import lax
from jax.experimental import pallas as pl
from jax.experimental.pallas import tpu as pltpu

F32 = jnp.float32
BF16 = jnp.bfloat16

EPS = 1e-6
NEG_INF = -1e30
LOG2_E = 1.0 / math.log(2.0)

D_HYENA = 512
D_SCONV = 512
N_HEADS = 16
N_KV_HEADS = 4
GROUP = N_HEADS // N_KV_HEADS
HEAD_DIM = 64
D_ATTN = N_HEADS * HEAD_DIM
D_KV = N_KV_HEADS * HEAD_DIM
WINDOW = 128
BLOCK = 128
FILTER_BANDS = 16
FILTER_HIDDEN = 64
HYENA_MIN_DECAY = math.log(1e-2) / 1.5
HYENA_MAX_DECAY = math.log(1e-2) / 0.3
HYENA_SHIFT = 0.05

V7X_VMEM_LIMIT_BYTES = 60 * 1024 * 1024
HALO_ROWS = 16
LANES = 128
F32_TILE_ROWS = 8

FFT_B = 32
FFT_D = FFT_B // 2
PERM_ROWS = FFT_B * HALO_ROWS
FFT_LANES = 256


def _params(*sem):
    return pltpu.CompilerParams(dimension_semantics=sem, vmem_limit_bytes=V7X_VMEM_LIMIT_BYTES)


def _rms(x, g):
    ms = jnp.mean(x * x, axis=-1, keepdims=True)
    return x * lax.rsqrt(ms + EPS) * g


CAST_BLOCK_BYTES = 6 * 1024 * 1024


def _cast_kernel(w_ref, o_ref):
    o_ref[...] = w_ref[...].astype(BF16)


def _cast_bf16(w, layer):
    _, r, c = w.shape
    rows = r
    while rows * c * 4 > CAST_BLOCK_BYTES and rows % 32 == 0:
        rows //= 2
    return pl.pallas_call(
        _cast_kernel,
        grid=(r // rows,),
        in_specs=[pl.BlockSpec((None, rows, c), lambda i: (layer, i, 0))],
        out_specs=pl.BlockSpec((rows, c), lambda i: (i, 0)),
        out_shape=jax.ShapeDtypeStruct((r, c), BF16),
        compiler_params=_params("parallel"),
        name="cast_bf16",
    )(w)


def _ffn_kernel(x_ref, g_ref, wg_ref, wu_ref, wd_ref, *rest, n_side):
    side_in, o_ref, side_out, h_ref = rest[:n_side], rest[n_side], rest[n_side + 1:2 * n_side + 1], rest[-1]

    @pl.when(pl.program_id(1) == 0)
    def _():
        x = x_ref[...]
        h_ref[...] = _rms(x, g_ref[...]).astype(BF16)
        o_ref[...] = x

    h = h_ref[...]
    gate = jnp.dot(h, wg_ref[...], preferred_element_type=F32)
    up = jnp.dot(h, wu_ref[...], preferred_element_type=F32)
    a = (gate * jax.nn.sigmoid(gate) * up * 0.5).astype(BF16)
    o_ref[...] += jnp.dot(a, wd_ref[...], preferred_element_type=F32)
    for src, dst in zip(side_in, side_out):
        dst[...] = src[...].astype(BF16)


def _cast_blocks(shape, ni, nj):
    r, c = shape
    br = r // ni
    assert r % ni == 0 and br % HALO_ROWS == 0
    ncb = nj
    while c % ncb or (c // ncb) % LANES:
        ncb -= 1
    return (br, c // ncb), (lambda i, j: (i, jnp.minimum(j, ncb - 1)))


def _ffn(x, g, wg, wu, wd, *, tm, tf, cast=()):
    n, d = x.shape
    ff = wg.shape[1]
    ni, nj = n // tm, ff // tf
    side_args, side_in_specs, side_out_specs, side_shapes = [], [], [], []
    for w, layer in cast:
        blk, idx = _cast_blocks(w.shape[1:], ni, nj)
        side_args.append(w)
        side_in_specs.append(pl.BlockSpec((None,) + blk, lambda i, j, idx=idx, layer=layer: (layer,) + idx(i, j)))
        side_out_specs.append(pl.BlockSpec(blk, idx))
        side_shapes.append(jax.ShapeDtypeStruct(w.shape[1:], BF16))
    out = pl.pallas_call(
        functools.partial(_ffn_kernel, n_side=len(side_args)),
        grid=(ni, nj),
        in_specs=[
            pl.BlockSpec((tm, d), lambda i, j: (i, 0)),
            pl.BlockSpec((1, d), lambda i, j: (0, 0)),
            pl.BlockSpec((d, tf), lambda i, j: (0, j)),
            pl.BlockSpec((d, tf), lambda i, j: (0, j)),
            pl.BlockSpec((tf, d), lambda i, j: (j, 0)),
        ] + side_in_specs,
        out_specs=[pl.BlockSpec((tm, d), lambda i, j: (i, 0))] + side_out_specs,
        out_shape=[jax.ShapeDtypeStruct((n, d), F32)] + side_shapes,
        scratch_shapes=[pltpu.VMEM((tm, d), BF16)],
        compiler_params=_params("parallel", "arbitrary"),
        name="ffn",
    )(x, g, wg, wu, wd, *side_args)
    return out[0], tuple(out[1:])


def _inproj_kernel(x_ref, g_ref, w_ref, o_ref, h_ref):
    @pl.when(pl.program_id(1) == 0)
    def _():
        h_ref[...] = _rms(x_ref[...], g_ref[...]).astype(BF16)

    o_ref[...] = jnp.dot(h_ref[...], w_ref[...], preferred_element_type=F32).astype(o_ref.dtype)


def _inproj(x, g, w, *, tm, tn):
    n, d = x.shape
    dp = w.shape[1]
    return pl.pallas_call(
        _inproj_kernel,
        grid=(n // tm, dp // tn),
        in_specs=[
            pl.BlockSpec((tm, d), lambda i, j: (i, 0)),
            pl.BlockSpec((1, d), lambda i, j: (0, 0)),
            pl.BlockSpec((d, tn), lambda i, j: (0, j)),
        ],
        out_specs=pl.BlockSpec((tm, tn), lambda i, j: (i, j)),
        out_shape=jax.ShapeDtypeStruct((n, dp), BF16),
        scratch_shapes=[pltpu.VMEM((tm, d), BF16)],
        compiler_params=_params("parallel", "arbitrary"),
        name="inproj",
    )(x, g, w)


def _fft_tables(seq):
    period = 2 * seq
    t1 = period // FFT_B
    na = seq // FFT_B
    b = jnp.arange(FFT_B, dtype=jnp.int32)[:, None, None]
    c = jnp.arange(t1, dtype=jnp.int32)[None, :, None]
    a = jnp.arange(na, dtype=jnp.int32)[None, None, :]
    ang_a = ((c * a) % t1).astype(F32) * (2.0 * math.pi / t1)
    ang_b = ((c * b) % period).astype(F32) * (2.0 * math.pi / period)
    cos_a, sin_a, cos_b, sin_b = jnp.cos(ang_a), jnp.sin(ang_a), jnp.cos(ang_b), jnp.sin(ang_b)
    f1 = jnp.concatenate([cos_a * cos_b - sin_a * sin_b, -(sin_a * cos_b + cos_a * sin_b)], axis=1)
    d = jnp.arange(FFT_D, dtype=jnp.int32)[:, None]
    bb = jnp.arange(FFT_B, dtype=jnp.int32)[None, :]
    ang2 = ((d * bb) % FFT_B).astype(F32) * (2.0 * math.pi / FFT_B)
    eye = jnp.eye(F32_TILE_ROWS, dtype=F32)
    cr = jnp.kron(jnp.cos(ang2), eye)
    sr = jnp.kron(jnp.sin(ang2), eye)
    mb = jnp.block([[cr, sr], [-sr, cr]])
    mbi = jnp.block([[cr.T, -sr.T], [sr.T, cr.T]])
    r = jnp.arange(PERM_ROWS, dtype=jnp.int32)
    src = (r % HALO_ROWS) * FFT_B + r // HALO_ROWS
    perm = (src[:, None] == r[None, :]).astype(BF16)
    return dict(f1=f1.astype(BF16), f1t=jnp.swapaxes(f1, 1, 2).astype(BF16), mb=mb.astype(BF16),
                mbi=mbi.astype(BF16), perm=perm, unperm=perm.T)


def _filter_kernel(z_ref, w1_ref, b1_ref, w2_ref, b2_ref, w3_ref, b3_ref, fr_ref, wo_ref, dl_ref, perm_ref,
                   kpm_ref, ny_ref, *, seq, rows):
    i = pl.program_id(0)
    hp = lax.Precision.HIGHEST
    fr = fr_ref[...]
    h = jnp.sin(fr * (jnp.dot(w1_ref[...], z_ref[...], precision=hp, preferred_element_type=F32) + b1_ref[...]))
    h = jnp.sin(fr * (jnp.dot(w2_ref[...], h, precision=hp, preferred_element_type=F32) + b2_ref[...]))
    h = jnp.sin(fr * (jnp.dot(w3_ref[...], h, precision=hp, preferred_element_type=F32) + b3_ref[...]))
    h = lax.dot_general(h, wo_ref[...], (((0,), (0,)), ((), ())), precision=hp,
                        preferred_element_type=F32)
    n = lax.broadcasted_iota(jnp.int32, (rows, 1), 0) + i * rows
    t = n.astype(F32) / float(max(seq - 1, 1))
    window = jnp.exp(-t * dl_ref[...]) + HYENA_SHIFT
    hf = h[:, :D_HYENA] * window
    hb = jnp.where(n == 0, 0.0, h[:, D_HYENA:] * window)
    kp = hf + hb
    perm = perm_ref[...]
    kpm_ref[0] = jnp.dot(perm, kp.astype(BF16), preferred_element_type=F32).astype(BF16)
    kpm_ref[1] = jnp.dot(perm, (hf - hb).astype(BF16), preferred_element_type=F32).astype(BF16)
    sign = jnp.where((n & 1) == 0, 1.0, -1.0)

    @pl.when(i == 0)
    def _():
        ny_ref[...] = jnp.zeros_like(ny_ref)

    ny_ref[...] += jnp.sum(kp * sign, axis=0, keepdims=True)


def _hyena_filter(z, w1, b1, w2, b2, w3, b3, fr, wo, deltas, perm):
    zw, seq = z.shape
    rows = PERM_ROWS
    full = lambda a: pl.BlockSpec(a.shape, lambda i: (0,) * a.ndim)
    return pl.pallas_call(
        functools.partial(_filter_kernel, seq=seq, rows=rows),
        grid=(seq // rows,),
        in_specs=[pl.BlockSpec((zw, rows), lambda i: (0, i))] + [
            full(a) for a in (w1, b1, w2, b2, w3, b3, fr, wo, deltas, perm)],
        out_specs=[
            pl.BlockSpec((2, rows, D_HYENA), lambda i: (0, i, 0)),
            pl.BlockSpec((1, D_HYENA), lambda i: (0, 0)),
        ],
        out_shape=[
            jax.ShapeDtypeStruct((2, seq, D_HYENA), BF16),
            jax.ShapeDtypeStruct((1, D_HYENA), F32),
        ],
        compiler_params=_params("arbitrary"),
        name="hyena_filter",
    )(z, w1, b1, w2, b2, w3, b3, fr, wo, deltas, perm)


def _shift_rows(u, halo_prev, halo_next, first, last):
    rows = u.shape[0]
    r = lax.broadcasted_iota(jnp.int32, (rows, 1), 0)
    prev_row = jnp.where(first, 0.0, halo_prev)
    next_row = jnp.where(last, 0.0, halo_next)
    um = jnp.where(r == 0, prev_row, pltpu.roll(u, 1, axis=0))
    up = jnp.where(r == rows - 1, next_row, pltpu.roll(u, rows - 1, axis=0))
    return um, up


def _mixprep_kernel(p_ref, pp_ref, pn_ref, hw_ref, hb_ref, sw_ref, perm_ref, zp_ref, x0p_ref, ysc_ref):
    first = pl.program_id(1) == 0
    last = pl.program_id(1) == pl.num_programs(1) - 1
    nh = 3 * D_HYENA
    hr = HALO_ROWS
    p = p_ref[:, :nh].astype(F32)
    pm, pp = _shift_rows(p, pp_ref[hr - 1:hr, :nh].astype(F32), pn_ref[0:1, :nh].astype(F32), first, last)
    u = pm * hw_ref[0:1, :] + p * hw_ref[1:2, :] + pp * hw_ref[2:3, :] + hb_ref[...]
    zv = u[:, 2 * D_HYENA:] * u[:, D_HYENA:2 * D_HYENA]
    perm = perm_ref[...]
    zp_ref[...] = jnp.dot(perm, zv.astype(BF16), preferred_element_type=F32).astype(BF16)
    x0p_ref[...] = jnp.dot(perm, u[:, :D_HYENA].astype(BF16), preferred_element_type=F32).astype(BF16)
    ds = D_SCONV
    gb = p_ref[:, nh:nh + ds].astype(F32)
    ch = p_ref[:, nh + ds:nh + 2 * ds].astype(F32) * p_ref[:, nh + 2 * ds:nh + 3 * ds].astype(F32)
    ch_prev = pp_ref[hr - 1:hr, nh + ds:nh + 2 * ds].astype(F32) * pp_ref[hr - 1:hr, nh + 2 * ds:nh + 3 * ds].astype(F32)
    ch_next = pn_ref[0:1, nh + ds:nh + 2 * ds].astype(F32) * pn_ref[0:1, nh + 2 * ds:nh + 3 * ds].astype(F32)
    cm, cp = _shift_rows(ch, ch_prev, ch_next, first, last)
    ysc_ref[...] = gb * (cm * sw_ref[0:1, :] + ch * sw_ref[1:2, :] + cp * sw_ref[2:3, :])


def _mixprep(p, hw, hb, sw, perm, *, batch, seq):
    n = batch * seq
    tl = PERM_ROWS
    nl = seq // tl
    width = 3 * D_HYENA + 3 * D_SCONV
    hb8 = tl // HALO_ROWS
    last8 = seq // HALO_ROWS - 1
    row = lambda b, l: (b * nl + l, 0)
    const = lambda a: pl.BlockSpec(a.shape, lambda b, l: (0, 0))
    return pl.pallas_call(
        _mixprep_kernel,
        grid=(batch, nl),
        in_specs=[
            pl.BlockSpec((tl, width), row),
            pl.BlockSpec((HALO_ROWS, width), lambda b, l: (b * (last8 + 1) + jnp.maximum(l * hb8 - 1, 0), 0)),
            pl.BlockSpec((HALO_ROWS, width), lambda b, l: (b * (last8 + 1) + jnp.minimum((l + 1) * hb8, last8), 0)),
            const(hw), const(hb), const(sw), const(perm),
        ],
        out_specs=[pl.BlockSpec((tl, D_HYENA), row)] * 3,
        out_shape=[
            jax.ShapeDtypeStruct((n, D_HYENA), BF16),
            jax.ShapeDtypeStruct((n, D_HYENA), BF16),
            jax.ShapeDtypeStruct((n, D_SCONV), F32),
        ],
        compiler_params=_params("parallel", "parallel"),
        name="mixprep",
    )(p, p, p, hw, hb, sw, perm)


def _fft_forward(zp_ref, f1_ref, mb_ref, a_scr, *, seq, emit):
    t1 = 2 * seq // FFT_B
    r8 = F32_TILE_ROWS
    for b in range(FFT_B):
        xb = jnp.concatenate([zp_ref[blk * PERM_ROWS + b * HALO_ROWS:blk * PERM_ROWS + (b + 1) * HALO_ROWS, :]
                              for blk in range(seq // PERM_ROWS)], axis=0)
        a_scr[b] = jnp.dot(f1_ref[b], xb, preferred_element_type=F32)
    mb = mb_ref[...]
    half = FFT_D * r8
    for ch in range(t1 // r8):
        re = [a_scr[b, ch * r8:(ch + 1) * r8, :] for b in range(FFT_B)]
        im = [a_scr[b, t1 + ch * r8:t1 + (ch + 1) * r8, :] for b in range(FFT_B)]
        xs = jnp.dot(mb, jnp.concatenate(re + im, axis=0).astype(BF16), preferred_element_type=F32)
        emit(ch, xs[:half], xs[half:])


def _filter_fft_kernel(kpm_ref, f1_ref, mb_ref, k_ref, a_scr, *, seq):
    part = pl.program_id(0)
    half = FFT_D * F32_TILE_ROWS

    def emit(ch, re, im):
        val = jnp.where(part == 0, re, im) * (1.0 / seq)
        if ch == 0:
            val = jnp.where(lax.broadcasted_iota(jnp.int32, (half, 1), 0) == 0, 0.5 * val, val)
        k_ref[ch * half:(ch + 1) * half, :] = val.astype(BF16)

    _fft_forward(kpm_ref, f1_ref, mb_ref, a_scr, seq=seq, emit=emit)


def _filter_fft(kpm, tabs):
    _, seq, width = kpm.shape
    w = FFT_LANES
    f1, mb = tabs["f1"], tabs["mb"]
    return pl.pallas_call(
        functools.partial(_filter_fft_kernel, seq=seq),
        grid=(2, width // w),
        in_specs=[
            pl.BlockSpec((None, seq, w), lambda p, h: (p, 0, h)),
            pl.BlockSpec(f1.shape, lambda p, h: (0, 0, 0)),
            pl.BlockSpec(mb.shape, lambda p, h: (0, 0)),
        ],
        out_specs=pl.BlockSpec((None, seq, w), lambda p, h: (p, 0, h)),
        out_shape=jax.ShapeDtypeStruct((2, seq, width), BF16),
        scratch_shapes=[pltpu.VMEM((FFT_B, f1.shape[1], w), F32)],
        compiler_params=_params("parallel", "parallel"),
        name="filter_fft",
    )(kpm, f1, mb)


def _hyena_fwd_kernel(zp_ref, f1_ref, mb_ref, k_ref, ny_ref, y_ref, yn_ref, a_scr, *, seq):
    half = FFT_D * F32_TILE_ROWS
    r = lax.broadcasted_iota(jnp.int32, (seq, 1), 0)
    sign = jnp.where(((r // HALO_ROWS) & 1) == 0, 1.0, -1.0)
    xn = jnp.sum(zp_ref[...].astype(F32) * sign, axis=0, keepdims=True)
    yn_ref[...] = (xn * ny_ref[...] * (0.5 / seq)).reshape(yn_ref.shape)

    def emit(ch, re, im):
        kr = k_ref[0, ch * half:(ch + 1) * half, :].astype(F32)
        ki = k_ref[1, ch * half:(ch + 1) * half, :].astype(F32)
        y_ref[2 * ch * half:(2 * ch + 1) * half, :] = (re * kr - im * ki).astype(BF16)
        y_ref[(2 * ch + 1) * half:(2 * ch + 2) * half, :] = (re * ki + im * kr).astype(BF16)

    _fft_forward(zp_ref, f1_ref, mb_ref, a_scr, seq=seq, emit=emit)


def _hyena_fwd(zp, k, ny, tabs, *, batch, seq):
    width = zp.shape[1]
    w = FFT_LANES
    f1, mb = tabs["f1"], tabs["mb"]
    return pl.pallas_call(
        functools.partial(_hyena_fwd_kernel, seq=seq),
        grid=(batch, width // w),
        in_specs=[
            pl.BlockSpec((seq, w), lambda b, h: (b, h)),
            pl.BlockSpec(f1.shape, lambda b, h: (0, 0, 0)),
            pl.BlockSpec(mb.shape, lambda b, h: (0, 0)),
            pl.BlockSpec((2, seq, w), lambda b, h: (0, 0, h)),
            pl.BlockSpec((1, w), lambda b, h: (0, h)),
        ],
        out_specs=[
            pl.BlockSpec((2 * seq, w), lambda b, h: (b, h)),
            pl.BlockSpec((1, 1, w), lambda b, h: (b, 0, h)),
        ],
        out_shape=[
            jax.ShapeDtypeStruct((batch * 2 * seq, width), BF16),
            jax.ShapeDtypeStruct((batch, 1, width), F32),
        ],
        scratch_shapes=[pltpu.VMEM((FFT_B, f1.shape[1], w), F32)],
        compiler_params=_params("parallel", "parallel"),
        name="hyena_fwd",
    )(zp, f1, mb, k, ny)


def _hyena_inv_kernel(y_ref, mbi_ref, f1t_ref, yn_ref, zp_ref, x0p_ref, bd_ref, o_ref, g_scr, *, seq):
    t1 = 2 * seq // FFT_B
    r8 = F32_TILE_ROWS
    nch = t1 // r8
    blk_rows = 2 * FFT_D * r8
    mbi = mbi_ref[...]
    for ch in range(nch):
        g_scr[ch] = jnp.dot(mbi, y_ref[ch * blk_rows:(ch + 1) * blk_rows, :], preferred_element_type=F32)
    half = FFT_B * r8
    yn = yn_ref[0]
    bd = bd_ref[...]
    for b in range(FFT_B):
        re = [g_scr[ch, b * r8:(b + 1) * r8, :] for ch in range(nch)]
        im = [g_scr[ch, half + b * r8:half + (b + 1) * r8, :] for ch in range(nch)]
        yb = jnp.dot(f1t_ref[b], jnp.concatenate(re + im, axis=0).astype(BF16), preferred_element_type=F32)
        nyq = yn if b % 2 == 0 else -yn
        for blk in range(seq // PERM_ROWS):
            rows = slice(blk * PERM_ROWS + b * HALO_ROWS, blk * PERM_ROWS + (b + 1) * HALO_ROWS)
            conv = yb[blk * HALO_ROWS:(blk + 1) * HALO_ROWS, :]
            o_ref[rows, :] = x0p_ref[rows, :].astype(F32) * (conv + nyq + zp_ref[rows, :].astype(F32) * bd)


def _hyena_inv(y, yn, zp, x0p, bd, tabs, *, batch, seq):
    width = y.shape[1]
    w = FFT_LANES
    f1t, mbi = tabs["f1t"], tabs["mbi"]
    blk = pl.BlockSpec((seq, w), lambda b, h: (b, h))
    return pl.pallas_call(
        functools.partial(_hyena_inv_kernel, seq=seq),
        grid=(batch, width // w),
        in_specs=[
            pl.BlockSpec((2 * seq, w), lambda b, h: (b, h)),
            pl.BlockSpec(mbi.shape, lambda b, h: (0, 0)),
            pl.BlockSpec(f1t.shape, lambda b, h: (0, 0, 0)),
            pl.BlockSpec((1, 1, w), lambda b, h: (b, 0, h)),
            blk, blk,
            pl.BlockSpec((1, w), lambda b, h: (0, h)),
        ],
        out_specs=blk,
        out_shape=jax.ShapeDtypeStruct((batch * seq, width), F32),
        scratch_shapes=[pltpu.VMEM((2 * seq // FFT_B // F32_TILE_ROWS, 2 * FFT_B * F32_TILE_ROWS, w), F32)],
        compiler_params=_params("parallel", "parallel"),
        name="hyena_inv",
    )(y, mbi, f1t, yn, zp, x0p, bd)


HEADS_PER_TILE = LANES // HEAD_DIM
ATTN_GROUPS = tuple(
    tuple(h for h in range(N_HEADS)
          if (h // GROUP) // HEADS_PER_TILE == tile and (h % HEADS_PER_TILE != (h // GROUP) % HEADS_PER_TILE) == swapped)
    for tile in range(N_KV_HEADS // HEADS_PER_TILE) for swapped in (False, True))
ATTN_GROUP_ROWS = len(ATTN_GROUPS[0]) * BLOCK


def _attn_kernel(q_ref, kp_ref, kc_ref, kn_ref, vp_ref, vc_ref, vn_ref, qg_ref, kg_ref, sink_ref, o_ref, bias_ref,
                 *, blocks):
    step = pl.program_id(1)
    last_step = pl.num_programs(1) - 1
    keys = 3 * BLOCK
    grows = ATTN_GROUP_ROWS

    @pl.when((pl.program_id(0) == 0) & (step == 0))
    def _():
        kj = lax.broadcasted_iota(jnp.int32, (keys, BLOCK), 0)
        qi = lax.broadcasted_iota(jnp.int32, (keys, BLOCK), 1)
        dist = jnp.abs(qi - (kj - BLOCK))
        distf = dist.astype(F32)
        for variant, (lo, hi) in enumerate(((BLOCK, keys), (0, keys), (0, 2 * BLOCK))):
            ok = (dist <= WINDOW) & (kj >= lo) & (kj < hi)
            for g, heads in enumerate(ATTN_GROUPS):
                for j, h in enumerate(heads):
                    slope = LOG2_E * 2.0 ** (-8.0 * (h + 1) / N_HEADS)
                    bias_ref[variant, g, :, j * BLOCK:(j + 1) * BLOCK] = jnp.where(ok, -slope * distf, NEG_INF)

    seg_r = lax.broadcasted_iota(jnp.int32, (D_KV, D_KV), 0) // HEAD_DIM
    seg_c = lax.broadcasted_iota(jnp.int32, (D_KV, D_KV), 1) // HEAD_DIM
    seg_mean = jnp.where(seg_r == seg_c, 1.0 / HEAD_DIM, 0.0).astype(BF16)

    def head_norm(x, g):
        ms = jnp.dot((x * x).astype(BF16), seg_mean, preferred_element_type=F32)
        return x * lax.rsqrt(ms + EPS) * g

    low = lax.broadcasted_iota(jnp.int32, (1, LANES), 1) < HEAD_DIM
    qg = jnp.concatenate([qg_ref[...]] * GROUP, axis=1) * (LOG2_E * HEAD_DIM ** -0.5)
    kg = jnp.concatenate([kg_ref[...]] * N_KV_HEADS, axis=1)
    k = jnp.concatenate([kp_ref[...], kc_ref[...], kn_ref[...]], axis=0).astype(F32)
    k = head_norm(k, kg)
    v = jnp.concatenate([vp_ref[...], vc_ref[...], vn_ref[...]], axis=0).astype(F32)
    k_var, v_var = [], []
    for t in range(D_KV // LANES):
        kt = k[:, t * LANES:(t + 1) * LANES]
        vt = v[:, t * LANES:(t + 1) * LANES]
        k_var += [kt.astype(BF16), pltpu.roll(kt, HEAD_DIM, axis=1).astype(BF16)]
        v_var += [vt.astype(BF16), pltpu.roll(vt, HEAD_DIM, axis=1).astype(BF16)]

    q_half = {}
    for c in range(N_KV_HEADS):
        qn = head_norm(q_ref[:, c * D_KV:(c + 1) * D_KV].astype(F32), qg)
        for u in range(D_KV // LANES):
            qt = qn[:, u * LANES:(u + 1) * LANES]
            tile = c * (D_KV // LANES) + u
            q_half[tile, 0] = jnp.where(low, qt, 0.0).astype(BF16)
            q_half[tile, 1] = jnp.where(low, 0.0, qt).astype(BF16)

    lane_head = lax.broadcasted_iota(jnp.int32, (1, grows), 1) // BLOCK
    sinks = []
    for heads in ATTN_GROUPS:
        srow = jnp.zeros((1, grows), F32)
        for j, h in enumerate(heads):
            srow = jnp.where(lane_head == j, sink_ref[0:1, h:h + 1], srow)
        sinks.append(srow * LOG2_E)

    for qb in range(blocks):
        rows = slice(qb * BLOCK, (qb + 1) * BLOCK)
        krows = slice(qb * BLOCK, qb * BLOCK + keys)
        if qb == 0:
            variant = jnp.where(step == 0, 0, 1)
        elif qb == blocks - 1:
            variant = jnp.where(step == last_step, 2, 1)
        else:
            variant = 1
        res = {}
        for g, heads in enumerate(ATTN_GROUPS):
            qs = jnp.concatenate([q_half[h // HEADS_PER_TILE, h % HEADS_PER_TILE][rows] for h in heads], axis=0)
            st = lax.dot_general(k_var[g][krows], qs, (((1,), (1,)), ((), ())), preferred_element_type=F32)
            logits = st + bias_ref[variant, g]
            m = jnp.maximum(jnp.max(logits, axis=0, keepdims=True), sinks[g])
            p = jnp.exp2(logits - m)
            denom = jnp.sum(p, axis=0, keepdims=True) + jnp.exp2(sinks[g] - m)
            ot = lax.dot_general(v_var[g][krows], p.astype(BF16), (((0,), (0,)), ((), ())),
                                 preferred_element_type=F32) / denom
            for j, h in enumerate(heads):
                res[h] = ot[:, j * BLOCK:(j + 1) * BLOCK].T
        for tile in range(N_HEADS // HEADS_PER_TILE):
            o_ref[rows, tile * LANES:(tile + 1) * LANES] = jnp.where(
                low, res[HEADS_PER_TILE * tile], res[HEADS_PER_TILE * tile + 1])


def _attention(p, qg, kg, sink, *, batch, seq, blocks):
    n = batch * seq
    nb = seq // BLOCK
    tq = blocks * BLOCK
    ns = seq // tq
    assert HEADS_PER_TILE == 2 and blocks >= 2 and seq % tq == 0
    q_col = (3 * D_HYENA + 3 * D_SCONV) // D_ATTN
    k_col = (3 * D_HYENA + 3 * D_SCONV + D_ATTN) // D_KV
    v_col = k_col + 1
    assert q_col * D_ATTN == 3 * D_HYENA + 3 * D_SCONV and k_col * D_KV == q_col * D_ATTN + D_ATTN

    def kv_specs(col):
        prev = pl.BlockSpec((BLOCK, D_KV), lambda b, i: (b * nb + jnp.maximum(i * blocks - 1, 0), col))
        cur = pl.BlockSpec((tq, D_KV), lambda b, i: (b * ns + i, col))
        nxt = pl.BlockSpec((BLOCK, D_KV), lambda b, i: (b * nb + jnp.minimum((i + 1) * blocks, nb - 1), col))
        return [prev, cur, nxt]

    return pl.pallas_call(
        functools.partial(_attn_kernel, blocks=blocks),
        grid=(batch, ns),
        in_specs=[pl.BlockSpec((tq, D_ATTN), lambda b, i: (b * ns + i, q_col))] + kv_specs(k_col) + kv_specs(v_col) + [
            pl.BlockSpec((1, HEAD_DIM), lambda b, i: (0, 0)),
            pl.BlockSpec((1, HEAD_DIM), lambda b, i: (0, 0)),
            pl.BlockSpec((1, N_HEADS), lambda b, i: (0, 0)),
        ],
        out_specs=pl.BlockSpec((tq, D_ATTN), lambda b, i: (b * ns + i, 0)),
        out_shape=jax.ShapeDtypeStruct((n, D_ATTN), F32),
        scratch_shapes=[pltpu.VMEM((3, len(ATTN_GROUPS), 3 * BLOCK, ATTN_GROUP_ROWS), F32)],
        compiler_params=_params("arbitrary", "arbitrary"),
        name="attention",
    )(p, p, p, p, p, p, p, qg, kg, sink)


def _outproj_kernel(x_ref, yh_ref, ys_ref, ya_ref, g_ref, w_ref, unperm_ref, o_ref):
    d1 = D_HYENA
    d2 = D_HYENA + D_SCONV
    nh = _rms(yh_ref[...], g_ref[:, :d1]).astype(BF16)
    nh = jnp.dot(unperm_ref[...], nh, preferred_element_type=F32).astype(BF16)
    ns = _rms(ys_ref[...], g_ref[:, d1:d2]).astype(BF16)
    na = _rms(ya_ref[...], g_ref[:, d2:]).astype(BF16)
    acc = jnp.dot(nh, w_ref[:d1, :], preferred_element_type=F32)
    acc += jnp.dot(ns, w_ref[d1:d2, :], preferred_element_type=F32)
    acc += jnp.dot(na, w_ref[d2:, :], preferred_element_type=F32)
    o_ref[...] = x_ref[...] + acc


def _outproj(x, yh, ys, ya, g, w, unperm):
    n, d = x.shape
    tm = PERM_ROWS
    row = lambda i: (i, 0)
    return pl.pallas_call(
        _outproj_kernel,
        grid=(n // tm,),
        in_specs=[
            pl.BlockSpec((tm, d), row),
            pl.BlockSpec((tm, D_HYENA), row),
            pl.BlockSpec((tm, D_SCONV), row),
            pl.BlockSpec((tm, D_ATTN), row),
            pl.BlockSpec((1, w.shape[0]), lambda i: (0, 0)),
            pl.BlockSpec(w.shape, lambda i: (0, 0)),
            pl.BlockSpec(unperm.shape, lambda i: (0, 0)),
        ],
        out_specs=pl.BlockSpec((tm, d), row),
        out_shape=jax.ShapeDtypeStruct((n, d), F32),
        compiler_params=_params("parallel"),
        name="outproj",
    )(x, yh, ys, ya, g, w, unperm)


def _tile(n, want):
    t = min(n, want)
    assert n % t == 0
    return t


def _filter_features(seq):
    n = jnp.arange(seq, dtype=F32)
    t = n / float(max(seq - 1, 1))
    bands = jnp.linspace(1e-4, FILTER_BANDS - 1, FILTER_BANDS, dtype=F32)
    ang = (2.0 * math.pi / seq) * n[:, None] * bands[None, :]
    z = jnp.concatenate([t[:, None], jnp.cos(ang), jnp.sin(ang)], axis=-1)
    return jnp.pad(z, ((0, 0), (0, LANES - z.shape[1]))).T


def kernel(x, norm_ffn1, ffn1_w_gate, ffn1_w_up, ffn1_w_down, norm_mix, w_in, hyena_short_w, hyena_short_b,
           filt_w1, filt_b1, filt_w2, filt_b2, filt_w3, filt_b3, filt_freq, filt_w_out, hyena_bias, sconv_w,
           q_norm_g, k_norm_g, attn_sink, mix_out_g, w_out, norm_ffn2, ffn2_w_gate, ffn2_w_up, ffn2_w_down):
    batch, seq, d = x.shape
    depth = w_in.shape[0]
    n = batch * seq
    assert seq % PERM_ROWS == 0
    tm = _tile(n, 1024)
    tf = _tile(ffn1_w_gate.shape[2], 512)
    tn = _tile(w_in.shape[2], 1536)

    xf = x.reshape(n, d)
    row = lambda a: a.reshape(1, -1)
    tabs = _fft_tables(seq)
    z = _filter_features(seq)
    deltas = jnp.abs(jnp.linspace(HYENA_MIN_DECAY, HYENA_MAX_DECAY, D_HYENA, dtype=F32)).reshape(1, -1)

    ffn1_f32 = (ffn1_w_gate, ffn1_w_up, ffn1_w_down)
    ffn2_f32 = (ffn2_w_gate, ffn2_w_up, ffn2_w_down)
    ffn_w = tuple(_cast_bf16(w, 0) for w in ffn1_f32)
    proj_cast = tuple((w, l) for l in range(depth) for w in (w_in, w_out))

    for l in range(depth):
        cast = tuple((w, l) for w in ffn2_f32) + (proj_cast if l == 0 else ())
        xf, side = _ffn(xf, row(norm_ffn1[l]), *ffn_w, tm=tm, tf=tf, cast=cast)
        ffn_w = side[:3]
        if l == 0:
            proj_w = side[3:]

        p = _inproj(xf, row(norm_mix[l]), proj_w[2 * l], tm=tm, tn=tn)

        col = lambda a: a.reshape(-1, 1)
        w1t = jnp.pad(filt_w1[l], ((0, z.shape[0] - filt_w1.shape[1]), (0, 0))).T
        kpm, ny = _hyena_filter(z, w1t, col(filt_b1[l]), filt_w2[l].T, col(filt_b2[l]), filt_w3[l].T, col(filt_b3[l]),
                                col(filt_freq[l]), filt_w_out[l], deltas, tabs["perm"])
        kf = _filter_fft(kpm, tabs)

        zp, x0p, y_sc = _mixprep(p, hyena_short_w[l], row(hyena_short_b[l]), sconv_w[l], tabs["perm"],
                                 batch=batch, seq=seq)
        y, yn = _hyena_fwd(zp, kf, ny, tabs, batch=batch, seq=seq)
        y_hy = _hyena_inv(y, yn, zp, x0p, row(hyena_bias[l]), tabs, batch=batch, seq=seq)

        y_at = _attention(p, row(q_norm_g[l]), row(k_norm_g[l]), row(attn_sink[l]), batch=batch, seq=seq,
                          blocks=_tile(seq // BLOCK, 16))

        xf = _outproj(xf, y_hy, y_sc, y_at, row(mix_out_g[l]), proj_w[2 * l + 1], tabs["unperm"])

        cast = tuple((w, l + 1) for w in ffn1_f32) if l + 1 < depth else ()
        xf, ffn_w = _ffn(xf, row(norm_ffn2[l]), *ffn_w, tm=tm, tf=tf, cast=cast)

    return xf.reshape(batch, seq, d)
```

```python
import functools
import math

import jax
import jax.numpy as jnp
from jax import lax
from jax.experimental import pallas as pl
from jax.experimental.pallas import tpu as pltpu

F32 = jnp.float32
BF16 = jnp.bfloat16

EPS = 1e-6
NEG_INF = -1e30
LOG2_E = 1.0 / math.log(2.0)

D_HYENA = 512
D_SCONV = 512
N_HEADS = 16
N_KV_HEADS = 4
GROUP = N_HEADS // N_KV_HEADS
HEAD_DIM = 64
D_ATTN = N_HEADS * HEAD_DIM
D_KV = N_KV_HEADS * HEAD_DIM
WINDOW = 128
BLOCK = 128
FILTER_BANDS = 16
FILTER_HIDDEN = 64
HYENA_MIN_DECAY = math.log(1e-2) / 1.5
HYENA_MAX_DECAY = math.log(1e-2) / 0.3
HYENA_SHIFT = 0.05

V7X_VMEM_LIMIT_BYTES = 60 * 1024 * 1024
HALO_ROWS = 16
LANES = 128
F32_TILE_ROWS = 8

FFT_B = 32
FFT_D = FFT_B // 2
PERM_ROWS = FFT_B * HALO_ROWS
FFT_LANES = 256


def _params(*sem):
    return pltpu.CompilerParams(dimension_semantics=sem, vmem_limit_bytes=V7X_VMEM_LIMIT_BYTES)


def _rms(x, g):
    ms = jnp.mean(x * x, axis=-1, keepdims=True)
    return x * lax.rsqrt(ms + EPS) * g


CAST_BLOCK_BYTES = 6 * 1024 * 1024


def _cast_kernel(w_ref, o_ref):
    o_ref[...] = w_ref[...].astype(BF16)


def _cast_bf16(w, layer):
    _, r, c = w.shape
    rows = r
    while rows * c * 4 > CAST_BLOCK_BYTES and rows % 32 == 0:
        rows //= 2
    return pl.pallas_call(
        _cast_kernel,
        grid=(r // rows,),
        in_specs=[pl.BlockSpec((None, rows, c), lambda i: (layer, i, 0))],
        out_specs=pl.BlockSpec((rows, c), lambda i: (i, 0)),
        out_shape=jax.ShapeDtypeStruct((r, c), BF16),
        compiler_params=_params("parallel"),
        name="cast_bf16",
    )(w)


def _ffn_kernel(x_ref, g_ref, wg_ref, wu_ref, wd_ref, *rest, n_side):
    side_in, o_ref, side_out, h_ref = rest[:n_side], rest[n_side], rest[n_side + 1:2 * n_side + 1], rest[-1]

    @pl.when(pl.program_id(1) == 0)
    def _():
        x = x_ref[...]
        h_ref[...] = _rms(x, g_ref[...]).astype(BF16)
        o_ref[...] = x

    h = h_ref[...]
    gate = jnp.dot(h, wg_ref[...], preferred_element_type=F32)
    up = jnp.dot(h, wu_ref[...], preferred_element_type=F32)
    a = (gate * jax.nn.sigmoid(gate) * up * 0.5).astype(BF16)
    o_ref[...] += jnp.dot(a, wd_ref[...], preferred_element_type=F32)
    for src, dst in zip(side_in, side_out):
        dst[...] = src[...].astype(BF16)


def _cast_blocks(shape, ni, nj):
    r, c = shape
    br = r // ni
    assert r % ni == 0 and br % HALO_ROWS == 0
    ncb = nj
    while c % ncb or (c // ncb) % LANES:
        ncb -= 1
    return (br, c // ncb), (lambda i, j: (i, jnp.minimum(j, ncb - 1)))


def _ffn(x, g, wg, wu, wd, *, tm, tf, cast=()):
    n, d = x.shape
    ff = wg.shape[1]
    ni, nj = n // tm, ff // tf
    side_args, side_in_specs, side_out_specs, side_shapes = [], [], [], []
    for w, layer in cast:
        blk, idx = _cast_blocks(w.shape[1:], ni, nj)
        side_args.append(w)
        side_in_specs.append(pl.BlockSpec((None,) + blk, lambda i, j, idx=idx, layer=layer: (layer,) + idx(i, j)))
        side_out_specs.append(pl.BlockSpec(blk, idx))
        side_shapes.append(jax.ShapeDtypeStruct(w.shape[1:], BF16))
    out = pl.pallas_call(
        functools.partial(_ffn_kernel, n_side=len(side_args)),
        grid=(ni, nj),
        in_specs=[
            pl.BlockSpec((tm, d), lambda i, j: (i, 0)),
            pl.BlockSpec((1, d), lambda i, j: (0, 0)),
            pl.BlockSpec((d, tf), lambda i, j: (0, j)),
            pl.BlockSpec((d, tf), lambda i, j: (0, j)),
            pl.BlockSpec((tf, d), lambda i, j: (j, 0)),
        ] + side_in_specs,
        out_specs=[pl.BlockSpec((tm, d), lambda i, j: (i, 0))] + side_out_specs,
        out_shape=[jax.ShapeDtypeStruct((n, d), F32)] + side_shapes,
        scratch_shapes=[pltpu.VMEM((tm, d), BF16)],
        compiler_params=_params("parallel", "arbitrary"),
        name="ffn",
    )(x, g, wg, wu, wd, *side_args)
    return out[0], tuple(out[1:])


def _inproj_kernel(x_ref, g_ref, w_ref, o_ref, h_ref):
    @pl.when(pl.program_id(1) == 0)
    def _():
        h_ref[...] = _rms(x_ref[...], g_ref[...]).astype(BF16)

    o_ref[...] = jnp.dot(h_ref[...], w_ref[...], preferred_element_type=F32).astype(o_ref.dtype)


def _inproj(x, g, w, *, tm, tn):
    n, d = x.shape
    dp = w.shape[1]
    return pl.pallas_call(
        _inproj_kernel,
        grid=(n // tm, dp // tn),
        in_specs=[
            pl.BlockSpec((tm, d), lambda i, j: (i, 0)),
            pl.BlockSpec((1, d), lambda i, j: (0, 0)),
            pl.BlockSpec((d, tn), lambda i, j: (0, j)),
        ],
        out_specs=pl.BlockSpec((tm, tn), lambda i, j: (i, j)),
        out_shape=jax.ShapeDtypeStruct((n, dp), BF16),
        scratch_shapes=[pltpu.VMEM((tm, d), BF16)],
        compiler_params=_params("parallel", "arbitrary"),
        name="inproj",
    )(x, g, w)


def _fft_tables(seq):
    period = 2 * seq
    t1 = period // FFT_B
    na = seq // FFT_B
    b = jnp.arange(FFT_B, dtype=jnp.int32)[:, None, None]
    c = jnp.arange(t1, dtype=jnp.int32)[None, :, None]
    a = jnp.arange(na, dtype=jnp.int32)[None, None, :]
    ang_a = ((c * a) % t1).astype(F32) * (2.0 * math.pi / t1)
    ang_b = ((c * b) % period).astype(F32) * (2.0 * math.pi / period)
    cos_a, sin_a, cos_b, sin_b = jnp.cos(ang_a), jnp.sin(ang_a), jnp.cos(ang_b), jnp.sin(ang_b)
    f1 = jnp.concatenate([cos_a * cos_b - sin_a * sin_b, -(sin_a * cos_b + cos_a * sin_b)], axis=1)
    d = jnp.arange(FFT_D, dtype=jnp.int32)[:, None]
    bb = jnp.arange(FFT_B, dtype=jnp.int32)[None, :]
    ang2 = ((d * bb) % FFT_B).astype(F32) * (2.0 * math.pi / FFT_B)
    eye = jnp.eye(F32_TILE_ROWS, dtype=F32)
    cr = jnp.kron(jnp.cos(ang2), eye)
    sr = jnp.kron(jnp.sin(ang2), eye)
    mb = jnp.block([[cr, sr], [-sr, cr]])
    mbi = jnp.block([[cr.T, -sr.T], [sr.T, cr.T]])
    r = jnp.arange(PERM_ROWS, dtype=jnp.int32)
    src = (r % HALO_ROWS) * FFT_B + r // HALO_ROWS
    perm = (src[:, None] == r[None, :]).astype(BF16)
    return dict(f1=f1.astype(BF16), f1t=jnp.swapaxes(f1, 1, 2).astype(BF16), mb=mb.astype(BF16),
                mbi=mbi.astype(BF16), perm=perm, unperm=perm.T)


def _filter_kernel(z_ref, w1_ref, b1_ref, w2_ref, b2_ref, w3_ref, b3_ref, fr_ref, wo_ref, dl_ref, perm_ref,
                   *rest, seq, rows, n_side):
    side_in, (kpm_ref, ny_ref), side_out = rest[:n_side], rest[n_side:n_side + 2], rest[n_side + 2:]
    for src, dst in zip(side_in, side_out):
        dst[...] = src[...].astype(BF16)
    i = pl.program_id(0)
    hp = lax.Precision.HIGHEST
    fr = fr_ref[...]
    h = jnp.sin(fr * (jnp.dot(w1_ref[...], z_ref[...], precision=hp, preferred_element_type=F32) + b1_ref[...]))
    h = jnp.sin(fr * (jnp.dot(w2_ref[...], h, precision=hp, preferred_element_type=F32) + b2_ref[...]))
    h = jnp.sin(fr * (jnp.dot(w3_ref[...], h, precision=hp, preferred_element_type=F32) + b3_ref[...]))
    h = lax.dot_general(h, wo_ref[...], (((0,), (0,)), ((), ())), precision=hp,
                        preferred_element_type=F32)
    n = lax.broadcasted_iota(jnp.int32, (rows, 1), 0) + i * rows
    t = n.astype(F32) / float(max(seq - 1, 1))
    window = jnp.exp(-t * dl_ref[...]) + HYENA_SHIFT
    hf = h[:, :D_HYENA] * window
    hb = jnp.where(n == 0, 0.0, h[:, D_HYENA:] * window)
    kp = hf + hb
    perm = perm_ref[...]
    kpm_ref[0] = jnp.dot(perm, kp.astype(BF16), preferred_element_type=F32).astype(BF16)
    kpm_ref[1] = jnp.dot(perm, (hf - hb).astype(BF16), preferred_element_type=F32).astype(BF16)
    sign = jnp.where((n & 1) == 0, 1.0, -1.0)

    @pl.when(i == 0)
    def _():
        ny_ref[...] = jnp.zeros_like(ny_ref)

    ny_ref[...] += jnp.sum(kp * sign, axis=0, keepdims=True)


def _hyena_filter(z, w1, b1, w2, b2, w3, b3, fr, wo, deltas, perm, cast=()):
    zw, seq = z.shape
    rows = PERM_ROWS
    steps = seq // rows
    full = lambda a: pl.BlockSpec(a.shape, lambda i: (0,) * a.ndim)
    side_in_specs, side_out_specs, side_shapes = [], [], []
    for w, layer in cast:
        _, r, c = w.shape
        assert r % steps == 0 and (r // steps) % HALO_ROWS == 0
        side_in_specs.append(pl.BlockSpec((None, r // steps, c), lambda i, layer=layer: (layer, i, 0)))
        side_out_specs.append(pl.BlockSpec((r // steps, c), lambda i: (i, 0)))
        side_shapes.append(jax.ShapeDtypeStruct((r, c), BF16))
    out = pl.pallas_call(
        functools.partial(_filter_kernel, seq=seq, rows=rows, n_side=len(cast)),
        grid=(steps,),
        in_specs=[pl.BlockSpec((zw, rows), lambda i: (0, i))] + [
            full(a) for a in (w1, b1, w2, b2, w3, b3, fr, wo, deltas, perm)] + side_in_specs,
        out_specs=[
            pl.BlockSpec((2, rows, D_HYENA), lambda i: (0, i, 0)),
            pl.BlockSpec((1, D_HYENA), lambda i: (0, 0)),
        ] + side_out_specs,
        out_shape=[
            jax.ShapeDtypeStruct((2, seq, D_HYENA), BF16),
            jax.ShapeDtypeStruct((1, D_HYENA), F32),
        ] + side_shapes,
        compiler_params=_params("arbitrary"),
        name="hyena_filter",
    )(z, w1, b1, w2, b2, w3, b3, fr, wo, deltas, perm, *[w for w, _ in cast])
    return out[0], out[1], tuple(out[2:])


def _shift_rows(u, halo_prev, halo_next, first, last):
    rows = u.shape[0]
    r = lax.broadcasted_iota(jnp.int32, (rows, 1), 0)
    prev_row = jnp.where(first, 0.0, halo_prev)
    next_row = jnp.where(last, 0.0, halo_next)
    um = jnp.where(r == 0, prev_row, pltpu.roll(u, 1, axis=0))
    up = jnp.where(r == rows - 1, next_row, pltpu.roll(u, rows - 1, axis=0))
    return um, up


def _mixprep_kernel(p_ref, pp_ref, pn_ref, hw_ref, hb_ref, sw_ref, perm_ref, zp_ref, x0p_ref, ysc_ref):
    first = pl.program_id(1) == 0
    last = pl.program_id(1) == pl.num_programs(1) - 1
    nh = 3 * D_HYENA
    hr = HALO_ROWS
    p = p_ref[:, :nh].astype(F32)
    pm, pp = _shift_rows(p, pp_ref[hr - 1:hr, :nh].astype(F32), pn_ref[0:1, :nh].astype(F32), first, last)
    u = pm * hw_ref[0:1, :] + p * hw_ref[1:2, :] + pp * hw_ref[2:3, :] + hb_ref[...]
    zv = u[:, 2 * D_HYENA:] * u[:, D_HYENA:2 * D_HYENA]
    perm = perm_ref[...]
    zp_ref[...] = jnp.dot(perm, zv.astype(BF16), preferred_element_type=F32).astype(BF16)
    x0p_ref[...] = jnp.dot(perm, u[:, :D_HYENA].astype(BF16), preferred_element_type=F32).astype(BF16)
    ds = D_SCONV
    gb = p_ref[:, nh:nh + ds].astype(F32)
    ch = p_ref[:, nh + ds:nh + 2 * ds].astype(F32) * p_ref[:, nh + 2 * ds:nh + 3 * ds].astype(F32)
    ch_prev = pp_ref[hr - 1:hr, nh + ds:nh + 2 * ds].astype(F32) * pp_ref[hr - 1:hr, nh + 2 * ds:nh + 3 * ds].astype(F32)
    ch_next = pn_ref[0:1, nh + ds:nh + 2 * ds].astype(F32) * pn_ref[0:1, nh + 2 * ds:nh + 3 * ds].astype(F32)
    cm, cp = _shift_rows(ch, ch_prev, ch_next, first, last)
    ysc_ref[...] = gb * (cm * sw_ref[0:1, :] + ch * sw_ref[1:2, :] + cp * sw_ref[2:3, :])


def _mixprep(p, hw, hb, sw, perm, *, batch, seq):
    n = batch * seq
    tl = PERM_ROWS
    nl = seq // tl
    width = 3 * D_HYENA + 3 * D_SCONV
    hb8 = tl // HALO_ROWS
    last8 = seq // HALO_ROWS - 1
    row = lambda b, l: (b * nl + l, 0)
    const = lambda a: pl.BlockSpec(a.shape, lambda b, l: (0, 0))
    return pl.pallas_call(
        _mixprep_kernel,
        grid=(batch, nl),
        in_specs=[
            pl.BlockSpec((tl, width), row),
            pl.BlockSpec((HALO_ROWS, width), lambda b, l: (b * (last8 + 1) + jnp.maximum(l * hb8 - 1, 0), 0)),
            pl.BlockSpec((HALO_ROWS, width), lambda b, l: (b * (last8 + 1) + jnp.minimum((l + 1) * hb8, last8), 0)),
            const(hw), const(hb), const(sw), const(perm),
        ],
        out_specs=[pl.BlockSpec((tl, D_HYENA), row)] * 3,
        out_shape=[
            jax.ShapeDtypeStruct((n, D_HYENA), BF16),
            jax.ShapeDtypeStruct((n, D_HYENA), BF16),
            jax.ShapeDtypeStruct((n, D_SCONV), F32),
        ],
        compiler_params=_params("parallel", "parallel"),
        name="mixprep",
    )(p, p, p, hw, hb, sw, perm)


def _fft_forward(zp_ref, f1_ref, mb_ref, a_scr, *, seq, emit):
    t1 = 2 * seq // FFT_B
    r8 = F32_TILE_ROWS
    for b in range(FFT_B):
        xb = jnp.concatenate([zp_ref[blk * PERM_ROWS + b * HALO_ROWS:blk * PERM_ROWS + (b + 1) * HALO_ROWS, :]
                              for blk in range(seq // PERM_ROWS)], axis=0)
        a_scr[b] = jnp.dot(f1_ref[b], xb, preferred_element_type=F32)
    mb = mb_ref[...]
    half = FFT_D * r8
    for ch in range(t1 // r8):
        re = [a_scr[b, ch * r8:(ch + 1) * r8, :] for b in range(FFT_B)]
        im = [a_scr[b, t1 + ch * r8:t1 + (ch + 1) * r8, :] for b in range(FFT_B)]
        xs = jnp.dot(mb, jnp.concatenate(re + im, axis=0).astype(BF16), preferred_element_type=F32)
        emit(ch, xs[:half], xs[half:])


def _filter_fft_kernel(kpm_ref, f1_ref, mb_ref, k_ref, a_scr, *, seq):
    part = pl.program_id(0)
    half = FFT_D * F32_TILE_ROWS

    def emit(ch, re, im):
        val = jnp.where(part == 0, re, im) * (1.0 / seq)
        if ch == 0:
            val = jnp.where(lax.broadcasted_iota(jnp.int32, (half, 1), 0) == 0, 0.5 * val, val)
        k_ref[ch * half:(ch + 1) * half, :] = val.astype(BF16)

    _fft_forward(kpm_ref, f1_ref, mb_ref, a_scr, seq=seq, emit=emit)


def _filter_fft(kpm, tabs):
    _, seq, width = kpm.shape
    w = FFT_LANES
    f1, mb = tabs["f1"], tabs["mb"]
    return pl.pallas_call(
        functools.partial(_filter_fft_kernel, seq=seq),
        grid=(2, width // w),
        in_specs=[
            pl.BlockSpec((None, seq, w), lambda p, h: (p, 0, h)),
            pl.BlockSpec(f1.shape, lambda p, h: (0, 0, 0)),
            pl.BlockSpec(mb.shape, lambda p, h: (0, 0)),
        ],
        out_specs=pl.BlockSpec((None, seq, w), lambda p, h: (p, 0, h)),
        out_shape=jax.ShapeDtypeStruct((2, seq, width), BF16),
        scratch_shapes=[pltpu.VMEM((FFT_B, f1.shape[1], w), F32)],
        compiler_params=_params("parallel", "parallel"),
        name="filter_fft",
    )(kpm, f1, mb)


def _hyena_fwd_kernel(zp_ref, f1_ref, mb_ref, k_ref, ny_ref, y_ref, yn_ref, a_scr, *, seq):
    half = FFT_D * F32_TILE_ROWS
    r = lax.broadcasted_iota(jnp.int32, (seq, 1), 0)
    sign = jnp.where(((r // HALO_ROWS) & 1) == 0, 1.0, -1.0)
    xn = jnp.sum(zp_ref[...].astype(F32) * sign, axis=0, keepdims=True)
    yn_ref[...] = (xn * ny_ref[...] * (0.5 / seq)).reshape(yn_ref.shape)

    def emit(ch, re, im):
        kr = k_ref[0, ch * half:(ch + 1) * half, :].astype(F32)
        ki = k_ref[1, ch * half:(ch + 1) * half, :].astype(F32)
        y_ref[2 * ch * half:(2 * ch + 1) * half, :] = (re * kr - im * ki).astype(BF16)
        y_ref[(2 * ch + 1) * half:(2 * ch + 2) * half, :] = (re * ki + im * kr).astype(BF16)

    _fft_forward(zp_ref, f1_ref, mb_ref, a_scr, seq=seq, emit=emit)


def _hyena_fwd(zp, k, ny, tabs, *, batch, seq):
    width = zp.shape[1]
    w = FFT_LANES
    f1, mb = tabs["f1"], tabs["mb"]
    return pl.pallas_call(
        functools.partial(_hyena_fwd_kernel, seq=seq),
        grid=(batch, width // w),
        in_specs=[
            pl.BlockSpec((seq, w), lambda b, h: (b, h)),
            pl.BlockSpec(f1.shape, lambda b, h: (0, 0, 0)),
            pl.BlockSpec(mb.shape, lambda b, h: (0, 0)),
            pl.BlockSpec((2, seq, w), lambda b, h: (0, 0, h)),
            pl.BlockSpec((1, w), lambda b, h: (0, h)),
        ],
        out_specs=[
            pl.BlockSpec((2 * seq, w), lambda b, h: (b, h)),
            pl.BlockSpec((1, 1, w), lambda b, h: (b, 0, h)),
        ],
        out_shape=[
            jax.ShapeDtypeStruct((batch * 2 * seq, width), BF16),
            jax.ShapeDtypeStruct((batch, 1, width), F32),
        ],
        scratch_shapes=[pltpu.VMEM((FFT_B, f1.shape[1], w), F32)],
        compiler_params=_params("parallel", "parallel"),
        name="hyena_fwd",
    )(zp, f1, mb, k, ny)


def _hyena_inv_kernel(y_ref, mbi_ref, f1t_ref, yn_ref, zp_ref, x0p_ref, bd_ref, o_ref, g_scr, *, seq):
    t1 = 2 * seq // FFT_B
    r8 = F32_TILE_ROWS
    nch = t1 // r8
    blk_rows = 2 * FFT_D * r8
    mbi = mbi_ref[...]
    for ch in range(nch):
        g_scr[ch] = jnp.dot(mbi, y_ref[ch * blk_rows:(ch + 1) * blk_rows, :], preferred_element_type=F32)
    half = FFT_B * r8
    yn = yn_ref[0]
    bd = bd_ref[...]
    for b in range(FFT_B):
        re = [g_scr[ch, b * r8:(b + 1) * r8, :] for ch in range(nch)]
        im = [g_scr[ch, half + b * r8:half + (b + 1) * r8, :] for ch in range(nch)]
        yb = jnp.dot(f1t_ref[b], jnp.concatenate(re + im, axis=0).astype(BF16), preferred_element_type=F32)
        nyq = yn if b % 2 == 0 else -yn
        for blk in range(seq // PERM_ROWS):
            rows = slice(blk * PERM_ROWS + b * HALO_ROWS, blk * PERM_ROWS + (b + 1) * HALO_ROWS)
            conv = yb[blk * HALO_ROWS:(blk + 1) * HALO_ROWS, :]
            o_ref[rows, :] = x0p_ref[rows, :].astype(F32) * (conv + nyq + zp_ref[rows, :].astype(F32) * bd)


def _hyena_inv(y, yn, zp, x0p, bd, tabs, *, batch, seq):
    width = y.shape[1]
    w = FFT_LANES
    f1t, mbi = tabs["f1t"], tabs["mbi"]
    blk = pl.BlockSpec((seq, w), lambda b, h: (b, h))
    return pl.pallas_call(
        functools.partial(_hyena_inv_kernel, seq=seq),
        grid=(batch, width // w),
        in_specs=[
            pl.BlockSpec((2 * seq, w), lambda b, h: (b, h)),
            pl.BlockSpec(mbi.shape, lambda b, h: (0, 0)),
            pl.BlockSpec(f1t.shape, lambda b, h: (0, 0, 0)),
            pl.BlockSpec((1, 1, w), lambda b, h: (b, 0, h)),
            blk, blk,
            pl.BlockSpec((1, w), lambda b, h: (0, h)),
        ],
        out_specs=blk,
        out_shape=jax.ShapeDtypeStruct((batch * seq, width), F32),
        scratch_shapes=[pltpu.VMEM((2 * seq // FFT_B // F32_TILE_ROWS, 2 * FFT_B * F32_TILE_ROWS, w), F32)],
        compiler_params=_params("parallel", "parallel"),
        name="hyena_inv",
    )(y, mbi, f1t, yn, zp, x0p, bd)


HEADS_PER_TILE = LANES // HEAD_DIM
ATTN_GROUPS = tuple(
    tuple(h for h in range(N_HEADS)
          if (h // GROUP) // HEADS_PER_TILE == tile and (h % HEADS_PER_TILE != (h // GROUP) % HEADS_PER_TILE) == swapped)
    for tile in range(N_KV_HEADS // HEADS_PER_TILE) for swapped in (False, True))
ATTN_GROUP_ROWS = len(ATTN_GROUPS[0]) * BLOCK


def _attn_kernel(q_ref, kp_ref, kc_ref, kn_ref, vp_ref, vc_ref, vn_ref, qg_ref, kg_ref, sink_ref, o_ref, bias_ref,
                 *, blocks):
    step = pl.program_id(1)
    last_step = pl.num_programs(1) - 1
    keys = 3 * BLOCK
    grows = ATTN_GROUP_ROWS

    @pl.when((pl.program_id(0) == 0) & (step == 0))
    def _():
        kj = lax.broadcasted_iota(jnp.int32, (keys, BLOCK), 0)
        qi = lax.broadcasted_iota(jnp.int32, (keys, BLOCK), 1)
        dist = jnp.abs(qi - (kj - BLOCK))
        distf = dist.astype(F32)
        for variant, (lo, hi) in enumerate(((BLOCK, keys), (0, keys), (0, 2 * BLOCK))):
            ok = (dist <= WINDOW) & (kj >= lo) & (kj < hi)
            for g, heads in enumerate(ATTN_GROUPS):
                for j, h in enumerate(heads):
                    slope = LOG2_E * 2.0 ** (-8.0 * (h + 1) / N_HEADS)
                    bias_ref[variant, g, :, j * BLOCK:(j + 1) * BLOCK] = jnp.where(ok, -slope * distf, NEG_INF)

    seg_r = lax.broadcasted_iota(jnp.int32, (D_KV, D_KV), 0) // HEAD_DIM
    seg_c = lax.broadcasted_iota(jnp.int32, (D_KV, D_KV), 1) // HEAD_DIM
    seg_mean = jnp.where(seg_r == seg_c, 1.0 / HEAD_DIM, 0.0).astype(BF16)

    def head_norm(x, g):
        ms = jnp.dot((x * x).astype(BF16), seg_mean, preferred_element_type=F32)
        return x * lax.rsqrt(ms + EPS) * g

    low = lax.broadcasted_iota(jnp.int32, (1, LANES), 1) < HEAD_DIM
    qg = jnp.concatenate([qg_ref[...]] * GROUP, axis=1) * (LOG2_E * HEAD_DIM ** -0.5)
    kg = jnp.concatenate([kg_ref[...]] * N_KV_HEADS, axis=1)
    k = jnp.concatenate([kp_ref[...], kc_ref[...], kn_ref[...]], axis=0).astype(F32)
    k = head_norm(k, kg)
    v = jnp.concatenate([vp_ref[...], vc_ref[...], vn_ref[...]], axis=0).astype(F32)
    k_var, v_var = [], []
    for t in range(D_KV // LANES):
        kt = k[:, t * LANES:(t + 1) * LANES]
        vt = v[:, t * LANES:(t + 1) * LANES]
        k_var += [kt.astype(BF16), pltpu.roll(kt, HEAD_DIM, axis=1).astype(BF16)]
        v_var += [vt.astype(BF16), pltpu.roll(vt, HEAD_DIM, axis=1).astype(BF16)]

    q_half = {}
    for c in range(N_KV_HEADS):
        qn = head_norm(q_ref[:, c * D_KV:(c + 1) * D_KV].astype(F32), qg)
        for u in range(D_KV // LANES):
            qt = qn[:, u * LANES:(u + 1) * LANES]
            tile = c * (D_KV // LANES) + u
            q_half[tile, 0] = jnp.where(low, qt, 0.0).astype(BF16)
            q_half[tile, 1] = jnp.where(low, 0.0, qt).astype(BF16)

    lane_head = lax.broadcasted_iota(jnp.int32, (1, grows), 1) // BLOCK
    sinks = []
    for heads in ATTN_GROUPS:
        srow = jnp.zeros((1, grows), F32)
        for j, h in enumerate(heads):
            srow = jnp.where(lane_head == j, sink_ref[0:1, h:h + 1], srow)
        sinks.append(srow * LOG2_E)

    for qb in range(blocks):
        rows = slice(qb * BLOCK, (qb + 1) * BLOCK)
        krows = slice(qb * BLOCK, qb * BLOCK + keys)
        if qb == 0:
            variant = jnp.where(step == 0, 0, 1)
        elif qb == blocks - 1:
            variant = jnp.where(step == last_step, 2, 1)
        else:
            variant = 1
        res = {}
        for g, heads in enumerate(ATTN_GROUPS):
            qs = jnp.concatenate([q_half[h // HEADS_PER_TILE, h % HEADS_PER_TILE][rows] for h in heads], axis=0)
            st = lax.dot_general(k_var[g][krows], qs, (((1,), (1,)), ((), ())), preferred_element_type=F32)
            logits = st + bias_ref[variant, g]
            m = jnp.maximum(jnp.max(logits, axis=0, keepdims=True), sinks[g])
            p = jnp.exp2(logits - m)
            denom = jnp.sum(p, axis=0, keepdims=True) + jnp.exp2(sinks[g] - m)
            ot = lax.dot_general(v_var[g][krows], p.astype(BF16), (((0,), (0,)), ((), ())),
                                 preferred_element_type=F32) / denom
            for j, h in enumerate(heads):
                res[h] = ot[:, j * BLOCK:(j + 1) * BLOCK].T
        for tile in range(N_HEADS // HEADS_PER_TILE):
            o_ref[rows, tile * LANES:(tile + 1) * LANES] = jnp.where(
                low, res[HEADS_PER_TILE * tile], res[HEADS_PER_TILE * tile + 1])


def _attention(p, qg, kg, sink, *, batch, seq, blocks):
    n = batch * seq
    nb = seq // BLOCK
    tq = blocks * BLOCK
    ns = seq // tq
    assert HEADS_PER_TILE == 2 and blocks >= 2 and seq % tq == 0
    q_col = (3 * D_HYENA + 3 * D_SCONV) // D_ATTN
    k_col = (3 * D_HYENA + 3 * D_SCONV + D_ATTN) // D_KV
    v_col = k_col + 1
    assert q_col * D_ATTN == 3 * D_HYENA + 3 * D_SCONV and k_col * D_KV == q_col * D_ATTN + D_ATTN

    def kv_specs(col):
        prev = pl.BlockSpec((BLOCK, D_KV), lambda b, i: (b * nb + jnp.maximum(i * blocks - 1, 0), col))
        cur = pl.BlockSpec((tq, D_KV), lambda b, i: (b * ns + i, col))
        nxt = pl.BlockSpec((BLOCK, D_KV), lambda b, i: (b * nb + jnp.minimum((i + 1) * blocks, nb - 1), col))
        return [prev, cur, nxt]

    return pl.pallas_call(
        functools.partial(_attn_kernel, blocks=blocks),
        grid=(batch, ns),
        in_specs=[pl.BlockSpec((tq, D_ATTN), lambda b, i: (b * ns + i, q_col))] + kv_specs(k_col) + kv_specs(v_col) + [
            pl.BlockSpec((1, HEAD_DIM), lambda b, i: (0, 0)),
            pl.BlockSpec((1, HEAD_DIM), lambda b, i: (0, 0)),
            pl.BlockSpec((1, N_HEADS), lambda b, i: (0, 0)),
        ],
        out_specs=pl.BlockSpec((tq, D_ATTN), lambda b, i: (b * ns + i, 0)),
        out_shape=jax.ShapeDtypeStruct((n, D_ATTN), F32),
        scratch_shapes=[pltpu.VMEM((3, len(ATTN_GROUPS), 3 * BLOCK, ATTN_GROUP_ROWS), F32)],
        compiler_params=_params("arbitrary", "arbitrary"),
        name="attention",
    )(p, p, p, p, p, p, p, qg, kg, sink)


def _outproj_kernel(x_ref, yh_ref, ys_ref, ya_ref, g_ref, w_ref, unperm_ref, o_ref):
    d1 = D_HYENA
    d2 = D_HYENA + D_SCONV
    nh = _rms(yh_ref[...], g_ref[:, :d1]).astype(BF16)
    nh = jnp.dot(unperm_ref[...], nh, preferred_element_type=F32).astype(BF16)
    ns = _rms(ys_ref[...], g_ref[:, d1:d2]).astype(BF16)
    na = _rms(ya_ref[...], g_ref[:, d2:]).astype(BF16)
    acc = jnp.dot(nh, w_ref[:d1, :], preferred_element_type=F32)
    acc += jnp.dot(ns, w_ref[d1:d2, :], preferred_element_type=F32)
    acc += jnp.dot(na, w_ref[d2:, :], preferred_element_type=F32)
    o_ref[...] = x_ref[...] + acc


def _outproj(x, yh, ys, ya, g, w, unperm):
    n, d = x.shape
    tm = PERM_ROWS
    row = lambda i: (i, 0)
    return pl.pallas_call(
        _outproj_kernel,
        grid=(n // tm,),
        in_specs=[
            pl.BlockSpec((tm, d), row),
            pl.BlockSpec((tm, D_HYENA), row),
            pl.BlockSpec((tm, D_SCONV), row),
            pl.BlockSpec((tm, D_ATTN), row),
            pl.BlockSpec((1, w.shape[0]), lambda i: (0, 0)),
            pl.BlockSpec(w.shape, lambda i: (0, 0)),
            pl.BlockSpec(unperm.shape, lambda i: (0, 0)),
        ],
        out_specs=pl.BlockSpec((tm, d), row),
        out_shape=jax.ShapeDtypeStruct((n, d), F32),
        compiler_params=_params("parallel"),
        name="outproj",
    )(x, yh, ys, ya, g, w, unperm)


def _tile(n, want):
    t = min(n, want)
    assert n % t == 0
    return t


def _filter_features(seq):
    n = jnp.arange(seq, dtype=F32)
    t = n / float(max(seq - 1, 1))
    bands = jnp.linspace(1e-4, FILTER_BANDS - 1, FILTER_BANDS, dtype=F32)
    ang = (2.0 * math.pi / seq) * n[:, None] * bands[None, :]
    z = jnp.concatenate([t[:, None], jnp.cos(ang), jnp.sin(ang)], axis=-1)
    return jnp.pad(z, ((0, 0), (0, LANES - z.shape[1]))).T


def kernel(x, norm_ffn1, ffn1_w_gate, ffn1_w_up, ffn1_w_down, norm_mix, w_in, hyena_short_w, hyena_short_b,
           filt_w1, filt_b1, filt_w2, filt_b2, filt_w3, filt_b3, filt_freq, filt_w_out, hyena_bias, sconv_w,
           q_norm_g, k_norm_g, attn_sink, mix_out_g, w_out, norm_ffn2, ffn2_w_gate, ffn2_w_up, ffn2_w_down):
    batch, seq, d = x.shape
    depth = w_in.shape[0]
    n = batch * seq
    assert seq % PERM_ROWS == 0
    tm = _tile(n, 1024)
    tf = _tile(ffn1_w_gate.shape[2], 512)
    tn = _tile(w_in.shape[2], 1536)

    xf = x.reshape(n, d)
    row = lambda a: a.reshape(1, -1)
    tabs = _fft_tables(seq)
    z = _filter_features(seq)
    deltas = jnp.abs(jnp.linspace(HYENA_MIN_DECAY, HYENA_MAX_DECAY, D_HYENA, dtype=F32)).reshape(1, -1)

    ffn1_f32 = (ffn1_w_gate, ffn1_w_up, ffn1_w_down)
    ffn2_f32 = (ffn2_w_gate, ffn2_w_up, ffn2_w_down)
    col = lambda a: a.reshape(-1, 1)
    filters, first_ffn = [], [None] * len(ffn1_f32)
    for l in range(depth):
        jobs = tuple(range(l, len(ffn1_f32), depth))
        w1t = jnp.pad(filt_w1[l], ((0, z.shape[0] - filt_w1.shape[1]), (0, 0))).T
        kpm, ny, side = _hyena_filter(z, w1t, col(filt_b1[l]), filt_w2[l].T, col(filt_b2[l]), filt_w3[l].T,
                                      col(filt_b3[l]), col(filt_freq[l]), filt_w_out[l], deltas, tabs["perm"],
                                      cast=tuple((ffn1_f32[k], 0) for k in jobs))
        for k, w in zip(jobs, side):
            first_ffn[k] = w
        filters.append((_filter_fft(kpm, tabs), ny))
    ffn_w = tuple(_cast_bf16(ffn1_f32[k], 0) if w is None else w for k, w in enumerate(first_ffn))
    proj_cast = tuple((w, l) for l in range(depth) for w in (w_in, w_out))

    for l in range(depth):
        cast = tuple((w, l) for w in ffn2_f32) + (proj_cast if l == 0 else ())
        xf, side = _ffn(xf, row(norm_ffn1[l]), *ffn_w, tm=tm, tf=tf, cast=cast)
        ffn_w = side[:3]
        if l == 0:
            proj_w = side[3:]

        p = _inproj(xf, row(norm_mix[l]), proj_w[2 * l], tm=tm, tn=tn)

        kf, ny = filters[l]
        zp, x0p, y_sc = _mixprep(p, hyena_short_w[l], row(hyena_short_b[l]), sconv_w[l], tabs["perm"],
                                 batch=batch, seq=seq)
        y, yn = _hyena_fwd(zp, kf, ny, tabs, batch=batch, seq=seq)
        y_hy = _hyena_inv(y, yn, zp, x0p, row(hyena_bias[l]), tabs, batch=batch, seq=seq)

        y_at = _attention(p, row(q_norm_g[l]), row(k_norm_g[l]), row(attn_sink[l]), batch=batch, seq=seq,
                          blocks=_tile(seq // BLOCK, 16))

        xf = _outproj(xf, y_hy, y_sc, y_at, row(mix_out_g[l]), proj_w[2 * l + 1], tabs["unperm"])

        cast = tuple((w, l + 1) for w in ffn1_f32) if l + 1 < depth else ()
        xf, ffn_w = _ffn(xf, row(norm_ffn2[l]), *ffn_w, tm=tm, tf=tf, cast=cast)

    return xf.reshape(batch, seq, d)
```
